```python
import jax
import jax.numpy as jnp
from jax import lax
import numpy as np

D_MODEL = 2048
BATCH = 4
SEQ = 8192
DEPTH = 4

GRID_W = 64
CTX_LEN = 256
N_MIXERS = 3
EPS = 1e-6
ROPE_BASE = 10000.0
QBLOCK = 128

GQA_HEADS = 32
GQA_KV_HEADS = 4
GQA_HEAD_DIM = 64
WINDOW = 128

POOL_WINDOWS = (2, 4, 8, 16)
POOL_GROUP = D_MODEL // len(POOL_WINDOWS)

MLA_HEADS = 16
MLA_Q_RANK = 512
MLA_KV_RANK = 512
MLA_NOPE = 128
MLA_ROPE = 64
MLA_V = 128

N_EXPERTS = 32
TOP_K = 4
EXPERT_FF = 768
SWIGLU_LIMIT = 7.0
SWIGLU_ALPHA = 1.702
EXPERT_BLOCK = 256

N_GQA_LAYERS = (DEPTH + 2) // N_MIXERS
N_POOL_LAYERS = (DEPTH + 1) // N_MIXERS
N_MLA_LAYERS = DEPTH // N_MIXERS

kernel_name = 'hybrid_dit_gqa_pool_mla_moe'


def rms_norm(x, g):
    xf = x.astype(jnp.float32)
    y = xf * lax.rsqrt(jnp.mean(xf * xf, axis=-1, keepdims=True) + EPS)
    return (y * g.astype(jnp.float32)).astype(x.dtype)


def modulate(h, shift, scale):
    return h * (1.0 + scale) + shift


def rope_1d(x, pos):
    half = x.shape[-1] // 2
    inv = ROPE_BASE ** (-jnp.arange(half, dtype=jnp.float32) / half)
    ang = pos.astype(jnp.float32)[:, None] * inv[None, :]
    cos = jnp.cos(ang)[None, :, None, :].astype(x.dtype)
    sin = jnp.sin(ang)[None, :, None, :].astype(x.dtype)
    x1, x2 = x[..., :half], x[..., half:]
    return jnp.concatenate([x1 * cos - x2 * sin, x2 * cos + x1 * sin], axis=-1)


def axial_rope(x, pos_r, pos_c):
    d = x.shape[-1] // 2
    return jnp.concatenate([rope_1d(x[..., :d], pos_r), rope_1d(x[..., d:], pos_c)], axis=-1)


def sink_softmax(sink_kg, scores):
    b, kv, g, q, _ = scores.shape
    col = jnp.broadcast_to(sink_kg[None, :, :, None, None], (b, kv, g, q, 1))
    return jax.nn.softmax(jnp.concatenate([col, scores], axis=-1), axis=-1)[..., 1:]


def gqa_window_mixer(h, hc, pos_r, pos_c, w_qkv, b_qkv, sink, w_o, b_o, need_ctx):
    B, L, _ = h.shape
    C = hc.shape[1]
    H, KV, Dh = GQA_HEADS, GQA_KV_HEADS, GQA_HEAD_DIM
    G = H // KV
    nq, nkv = H * Dh, KV * Dh
    scale = Dh ** -0.5
    sink_kg = sink.astype(jnp.float32).reshape(KV, G)

    def project(t):
        bn, n, _ = t.shape
        z = t @ w_qkv + b_qkv
        q = z[..., :nq].reshape(bn, n, H, Dh)
        k = z[..., nq:nq + nkv].reshape(bn, n, KV, Dh)
        v = z[..., nq + nkv:].reshape(bn, n, KV, Dh)
        return q, k, v

    q, k, v = project(h)
    qc, kc, vc = project(hc)
    q = axial_rope(q, pos_r, pos_c).reshape(B, L, KV, G, Dh)
    k = axial_rope(k, pos_r, pos_c)
    nb = L // QBLOCK
    pad = ((0, 0), (QBLOCK, QBLOCK), (0, 0), (0, 0))
    kp, vp = jnp.pad(k, pad), jnp.pad(v, pad)
    qb = jnp.moveaxis(q.reshape(B, nb, QBLOCK, KV, G, Dh), 1, 0)
    r = jnp.arange(QBLOCK)
    s = jnp.arange(3 * QBLOCK)

    def block(args):
        n, q_n = args
        start = n * QBLOCK
        k_n = lax.dynamic_slice_in_dim(kp, start, 3 * QBLOCK, axis=1)
        v_n = lax.dynamic_slice_in_dim(vp, start, 3 * QBLOCK, axis=1)
        i = start + r
        j = start - QBLOCK + s
        valid = (jnp.abs(i[:, None] - j[None, :]) <= WINDOW) & (j >= 0)[None, :] & (j < L)[None, :]
        s_loc = jnp.einsum('bqkgd,bskd->bkgqs', q_n, k_n).astype(jnp.float32) * scale
        s_loc = jnp.where(valid, s_loc, -jnp.inf)
        s_ctx = jnp.einsum('bqkgd,bckd->bkgqc', q_n, kc).astype(jnp.float32) * scale
        p = sink_softmax(sink_kg, jnp.concatenate([s_ctx, s_loc], axis=-1)).astype(v.dtype)
        return (jnp.einsum('bkgqc,bckd->bqkgd', p[..., :C], vc)
                + jnp.einsum('bkgqs,bskd->bqkgd', p[..., C:], v_n))

    o = lax.map(block, (jnp.arange(nb), qb))
    y = jnp.moveaxis(o, 0, 1).reshape(B, L, nq) @ w_o + b_o
    if not need_ctx:
        return y, None
    qc = qc.reshape(B, C, KV, G, Dh)
    s_cc = jnp.einsum('bqkgd,bckd->bkgqc', qc, kc).astype(jnp.float32) * scale
    p = sink_softmax(sink_kg, s_cc).astype(vc.dtype)
    yc = jnp.einsum('bkgqc,bckd->bqkgd', p, vc).reshape(B, C, nq) @ w_o + b_o
    return y, yc


def pool_mixer(h, hc, w, b, scale, need_ctx):
    def mix(t):
        n = t.shape[1]
        tf = t.astype(jnp.float32)
        csum = jnp.pad(jnp.cumsum(tf, axis=1), ((0, 0), (1, 0), (0, 0)))
        pos = jnp.arange(n)
        outs = []
        for g, win in enumerate(POOL_WINDOWS):
            lo = jnp.clip(pos - win // 2, 0, n)
            hi = jnp.clip(pos + win // 2, 0, n)
            sl = slice(g * POOL_GROUP, (g + 1) * POOL_GROUP)
            cg = csum[..., sl]
            mean = (cg[:, hi] - cg[:, lo]) / (hi - lo).astype(jnp.float32)[None, :, None]
            d = (mean - tf[..., sl]).astype(t.dtype)
            outs.append(d @ w[g] + b[sl])
        return jnp.concatenate(outs, axis=-1) * scale
    return mix(h), (mix(hc) if need_ctx else None)


def mla_mixer(h, hc, pos_r, pos_c, w_down, g_q, w_uq, g_kv, w_ukv, w_o, need_ctx):
    B, L, _ = h.shape
    C = hc.shape[1]
    H = MLA_HEADS
    scale = (MLA_NOPE + MLA_ROPE) ** -0.5

    def project(t, rotate):
        bn, n, _ = t.shape
        z = t @ w_down
        c_q = rms_norm(z[..., :MLA_Q_RANK], g_q)
        c_kv = rms_norm(z[..., MLA_Q_RANK:MLA_Q_RANK + MLA_KV_RANK], g_kv)
        k_rope = z[..., MLA_Q_RANK + MLA_KV_RANK:][:, :, None, :]
        q = (c_q @ w_uq).reshape(bn, n, H, MLA_NOPE + MLA_ROPE)
        q_nope, q_rope = q[..., :MLA_NOPE], q[..., MLA_NOPE:]
        kv = (c_kv @ w_ukv).reshape(bn, n, H, MLA_NOPE + MLA_V)
        k_nope, v = kv[..., :MLA_NOPE], kv[..., MLA_NOPE:]
        if rotate:
            q_rope = axial_rope(q_rope, pos_r, pos_c)
            k_rope = axial_rope(k_rope, pos_r, pos_c)
        return q_nope, q_rope, k_nope, k_rope[:, :, 0], v

    def attend(q_nope, q_rope, k_nope, k_rope, v):
        s = (jnp.einsum('bqhd,bshd->bhqs', q_nope, k_nope)
             + jnp.einsum('bqhr,bsr->bhqs', q_rope, k_rope)).astype(jnp.float32) * scale
        p = jax.nn.softmax(s, axis=-1).astype(v.dtype)
        return jnp.einsum('bhqs,bshd->bqhd', p, v)

    qn, qr, kn, kr, v = project(h, True)
    cqn, cqr, ckn, ckr, cv = project(hc, False)
    kn_all = jnp.concatenate([ckn, kn], axis=1)
    kr_all = jnp.concatenate([ckr, kr], axis=1)
    v_all = jnp.concatenate([cv, v], axis=1)
    nb = L // QBLOCK
    qnb = jnp.moveaxis(qn.reshape(B, nb, QBLOCK, H, MLA_NOPE), 1, 0)
    qrb = jnp.moveaxis(qr.reshape(B, nb, QBLOCK, H, MLA_ROPE), 1, 0)
    o = lax.map(lambda a: attend(a[0], a[1], kn_all, kr_all, v_all), (qnb, qrb))
    y = jnp.moveaxis(o, 0, 1).reshape(B, L, H * MLA_V) @ w_o
    if not need_ctx:
        return y, None
    yc = attend(cqn, cqr, ckn, ckr, cv).reshape(B, C, H * MLA_V) @ w_o
    return y, yc


def moe_ffn(h, w_r, b_r, w1, b1, w2, b2):
    T, D = h.shape
    E, K = N_EXPERTS, TOP_K
    logits = (h @ w_r + b_r).astype(jnp.float32)
    top_v, top_i = lax.top_k(logits, K)
    gate = jax.nn.softmax(top_v, axis=-1)
    n = T * K
    e_flat = top_i.reshape(n)
    order = jnp.argsort(e_flat)
    e_sorted = e_flat[order]
    tok_sorted = (order // K).astype(jnp.int32)
    gate_sorted = gate.reshape(n)[order]
    sizes = jnp.bincount(e_flat, length=E)
    starts = jnp.cumsum(sizes) - sizes
    padded = (sizes + EXPERT_BLOCK - 1) // EXPERT_BLOCK * EXPERT_BLOCK
    pends = jnp.cumsum(padded)
    pstarts = pends - padded
    dest = pstarts[e_sorted] + jnp.arange(n) - starts[e_sorted]
    n_blocks = -(-n // EXPERT_BLOCK) + E
    buf_tok = jnp.full((n_blocks * EXPERT_BLOCK,), T, jnp.int32).at[dest].set(tok_sorted)
    buf_gate = jnp.zeros((n_blocks * EXPERT_BLOCK,), jnp.float32).at[dest].set(gate_sorted)
    block_e = jnp.minimum(jnp.searchsorted(pends, jnp.arange(n_blocks) * EXPERT_BLOCK, side='right'), E - 1)
    h_pad = jnp.concatenate([h, jnp.zeros((1, D), h.dtype)], axis=0)

    def step(acc, blk):
        tok, g, e = blk
        a = h_pad[tok] @ w1[e] + b1[e]
        a_glu = jnp.minimum(a[:, :EXPERT_FF], SWIGLU_LIMIT)
        a_lin = jnp.clip(a[:, EXPERT_FF:], -SWIGLU_LIMIT, SWIGLU_LIMIT)
        u = a_glu * jax.nn.sigmoid(SWIGLU_ALPHA * a_glu) * (a_lin + 1.0)
        o = u @ w2[e] + b2[e]
        return acc.at[tok].add(g[:, None] * o.astype(jnp.float32)), None

    acc, _ = lax.scan(step, jnp.zeros((T + 1, D), jnp.float32),
                      (buf_tok.reshape(n_blocks, EXPERT_BLOCK), buf_gate.reshape(n_blocks, EXPERT_BLOCK), block_e))
    return acc[:T].astype(h.dtype)


def setup_inputs(seed: int = 0) -> dict:
    key = jax.random.key(seed)
    ks = jax.random.split(key, 32)
    f32 = jnp.float32
    D = D_MODEL

    def nrm(k, shape, fan_in, gain=1.0):
        return jax.random.normal(k, shape, f32) * (gain * fan_in ** -0.5)

    def noise(k, shape, s):
        return jax.random.normal(k, shape, f32) * s

    gqa_cols = GQA_HEADS * GQA_HEAD_DIM + 2 * GQA_KV_HEADS * GQA_HEAD_DIM
    return {
        'x': jax.random.normal(ks[0], (BATCH, SEQ, D), f32),
        'c': jax.random.normal(ks[1], (BATCH, D), f32),
        'ctx': jax.random.normal(ks[2], (BATCH, CTX_LEN, D), f32),
        'c_ctx': jax.random.normal(ks[3], (D,), f32),
        'ada_w': nrm(ks[4], (DEPTH, D, 6 * D), D, 0.5),
        'ada_b': noise(ks[5], (DEPTH, 6 * D), 0.02),
        'norm_g': 1.0 + noise(ks[6], (DEPTH, 2, D), 0.05),
        'final_g': 1.0 + noise(ks[7], (D,), 0.05),
        'gqa_w_qkv': nrm(ks[8], (N_GQA_LAYERS, D, gqa_cols), D),
        'gqa_b_qkv': noise(ks[9], (N_GQA_LAYERS, gqa_cols), 0.02),
        'gqa_sink': noise(ks[10], (N_GQA_LAYERS, GQA_HEADS), 1.0),
        'gqa_w_o': nrm(ks[11], (N_GQA_LAYERS, GQA_HEADS * GQA_HEAD_DIM, D), GQA_HEADS * GQA_HEAD_DIM),
        'gqa_b_o': noise(ks[12], (N_GQA_LAYERS, D), 0.02),
        'pool_w': nrm(ks[13], (N_POOL_LAYERS, len(POOL_WINDOWS), POOL_GROUP, POOL_GROUP), POOL_GROUP),
        'pool_b': noise(ks[14], (N_POOL_LAYERS, D), 0.02),
        'pool_scale': 1.0 + noise(ks[15], (N_POOL_LAYERS, D), 0.1),
        'mla_w_down': nrm(ks[16], (N_MLA_LAYERS, D, MLA_Q_RANK + MLA_KV_RANK + MLA_ROPE), D),
        'mla_g_q': 1.0 + noise(ks[17], (N_MLA_LAYERS, MLA_Q_RANK), 0.05),
        'mla_w_uq': nrm(ks[18], (N_MLA_LAYERS, MLA_Q_RANK, MLA_HEADS * (MLA_NOPE + MLA_ROPE)), MLA_Q_RANK),
        'mla_g_kv': 1.0 + noise(ks[19], (N_MLA_LAYERS, MLA_KV_RANK), 0.05),
        'mla_w_ukv': nrm(ks[20], (N_MLA_LAYERS, MLA_KV_RANK, MLA_HEADS * (MLA_NOPE + MLA_V)), MLA_KV_RANK),
        'mla_w_o': nrm(ks[21], (N_MLA_LAYERS, MLA_HEADS * MLA_V, D), MLA_HEADS * MLA_V),
        'router_w': nrm(ks[22], (DEPTH, D, N_EXPERTS), D),
        'router_b': noise(ks[23], (DEPTH, N_EXPERTS), 0.01),
        'exp_w1': nrm(ks[24], (DEPTH, N_EXPERTS, D, 2 * EXPERT_FF), D),
        'exp_b1': noise(ks[25], (DEPTH, N_EXPERTS, 2 * EXPERT_FF), 0.02),
        'exp_w2': nrm(ks[26], (DEPTH, N_EXPERTS, EXPERT_FF, D), EXPERT_FF),
        'exp_b2': noise(ks[27], (DEPTH, N_EXPERTS, D), 0.02),
    }


def reference(x, c, ctx, c_ctx, ada_w, ada_b, norm_g, final_g,
              gqa_w_qkv, gqa_b_qkv, gqa_sink, gqa_w_o, gqa_b_o,
              pool_w, pool_b, pool_scale,
              mla_w_down, mla_g_q, mla_w_uq, mla_g_kv, mla_w_ukv, mla_w_o,
              router_w, router_b, exp_w1, exp_b1, exp_w2, exp_b2):
    B, L, D = x.shape
    C = ctx.shape[1]
    rows = L // GRID_W
    pos_r = jnp.repeat(jnp.arange(rows), GRID_W)
    pos_c = jnp.tile(jnp.arange(GRID_W), rows)
    sc = jax.nn.silu(c)
    sc_ctx = jax.nn.silu(c_ctx)
    xc = ctx
    for i in range(DEPTH):
        need_ctx = i < DEPTH - 1
        mod = (sc @ ada_w[i] + ada_b[i])[:, None, :]
        mod_c = (sc_ctx @ ada_w[i] + ada_b[i])[None, None, :]
        sh1, s1, g1, sh2, s2, g2 = jnp.split(mod, 6, axis=-1)
        csh1, cs1, cg1, csh2, cs2, cg2 = jnp.split(mod_c, 6, axis=-1)
        h = modulate(rms_norm(x, norm_g[i, 0]), sh1, s1)
        hc = modulate(rms_norm(xc, norm_g[i, 0]), csh1, cs1)
        kind, j = i % N_MIXERS, i // N_MIXERS
        if kind == 0:
            y, yc = gqa_window_mixer(h, hc, pos_r, pos_c, gqa_w_qkv[j], gqa_b_qkv[j], gqa_sink[j],
                                     gqa_w_o[j], gqa_b_o[j], need_ctx)
        elif kind == 1:
            y, yc = pool_mixer(h, hc, pool_w[j], pool_b[j], pool_scale[j], need_ctx)
        else:
            y, yc = mla_mixer(h, hc, pos_r, pos_c, mla_w_down[j], mla_g_q[j], mla_w_uq[j],
                              mla_g_kv[j], mla_w_ukv[j], mla_w_o[j], need_ctx)
        x = x + g1 * y
        h = modulate(rms_norm(x, norm_g[i, 1]), sh2, s2).reshape(B * L, D)
        if need_ctx:
            xc = xc + cg1 * yc
            hc = modulate(rms_norm(xc, norm_g[i, 1]), csh2, cs2).reshape(B * C, D)
            out = moe_ffn(jnp.concatenate([h, hc], axis=0), router_w[i], router_b[i],
                          exp_w1[i], exp_b1[i], exp_w2[i], exp_b2[i])
            x = x + g2 * out[:B * L].reshape(B, L, D)
            xc = xc + cg2 * out[B * L:].reshape(B, C, D)
        else:
            out = moe_ffn(h, router_w[i], router_b[i], exp_w1[i], exp_b1[i], exp_w2[i], exp_b2[i])
            x = x + g2 * out.reshape(B, L, D)
    return rms_norm(x, final_g)
```

```python
import functools

import jax
import jax.numpy as jnp
from jax import lax
from jax.experimental import pallas as pl
from jax.experimental.pallas import tpu as pltpu

F32 = jnp.float32
BF16 = jnp.bfloat16
I32 = jnp.int32

GRID_W = 64
N_MIXERS = 3
EPS = 1e-6
ROPE_BASE = 10000.0

GQA_HEADS = 32
GQA_KV_HEADS = 4
GQA_HEAD_DIM = 64
GQA_GROUP = GQA_HEADS // GQA_KV_HEADS
WINDOW = 128

POOL_WINDOWS = (2, 4, 8, 16)
POOL_HALO = 8

MLA_HEADS = 16
MLA_Q_RANK = 512
MLA_KV_RANK = 512
MLA_NOPE = 128
MLA_ROPE = 64
MLA_V = 128
MLA_QK = MLA_NOPE + MLA_ROPE

N_EXPERTS = 32
TOP_K = 4
EXPERT_FF = 768
SWIGLU_LIMIT = 7.0
SWIGLU_ALPHA = 1.702

LANES = 128
UNIT = 16
SUB = 256
SUB_UNITS = (SUB * TOP_K + N_EXPERTS * (UNIT - 1) + UNIT - 1) // UNIT
SUB_ROWS = SUB_UNITS * UNIT
EXPERT_BLOCK = 256
BLOCK_UNITS = EXPERT_BLOCK // UNIT
VMEM_LIMIT = 56 * 1024 * 1024


def _cparams(*sem):
    return pltpu.CompilerParams(dimension_semantics=sem, vmem_limit_bytes=VMEM_LIMIT)


def _dot(a, b):
    return jnp.dot(a, b, preferred_element_type=F32)


def _dot_nt(a, b):
    return lax.dot_general(a, b, (((1,), (1,)), ((), ())), preferred_element_type=F32)


def _norm_mod(x, g, shift, scale):
    y = x * lax.rsqrt(jnp.mean(x * x, axis=-1, keepdims=True) + EPS) * g
    return y * (1.0 + scale) + shift


def _rope128(z, cos, sin):
    lane = lax.broadcasted_iota(I32, (1, LANES), 1)
    partner = jnp.where((lane % 32) < 16, pltpu.roll(z, LANES - 16, 1), pltpu.roll(z, 16, 1))
    return z * cos + partner * sin


def _ada_body(c_ref, w_ref, b_ref, o_ref):
    c = c_ref[...]
    s = (c * jax.nn.sigmoid(c)).astype(BF16)
    o_ref[0] = _dot(s, w_ref[0].astype(BF16)) + b_ref[0]


def _ada_mod(cc, ada_w, ada_b):
    depth, d, n = ada_w.shape
    tn = 1024
    return pl.pallas_call(
        _ada_body,
        grid=(depth, n // tn),
        in_specs=[pl.BlockSpec((UNIT, d), lambda i, j: (0, 0)),
                  pl.BlockSpec((1, d, tn), lambda i, j: (i, 0, j)),
                  pl.BlockSpec((1, 1, tn), lambda i, j: (i, 0, j))],
        out_specs=pl.BlockSpec((1, UNIT, tn), lambda i, j: (i, 0, j)),
        out_shape=jax.ShapeDtypeStruct((depth, UNIT, n), F32),
        compiler_params=_cparams("parallel", "parallel"),
        name="ada_mod",
    )(cc, ada_w, ada_b.reshape(depth, 1, n))


class _Dims:
    def __init__(self, B, L, C, D):
        self.B, self.L, self.C, self.D = B, L, C, D
        self.n_lat = B * L
        self.T = B * L + B * C

    def mod_row(self, tm):
        return lambda i: jnp.minimum((i * tm) // self.L, self.B)


def _mod_spec(dims, tm, chunk, d):
    row = dims.mod_row(tm)
    return pl.BlockSpec((1, 1, d), lambda i: (row(i) * 6 + chunk, 0, 0))


def _row_spec(tm, n):
    return pl.BlockSpec((tm, n), lambda i: (i, 0))


def _const_spec(shape):
    return pl.BlockSpec(shape, lambda i: (0,) * len(shape))


def _gqa_qkv_body(x_ref, g_ref, sh_ref, sc_ref, w_ref, b_ref, cos_ref, sin_ref, q_ref, k_ref, v_ref):
    h = _norm_mod(x_ref[...], g_ref[...], sh_ref[0], sc_ref[0]).astype(BF16)
    cos, sin = cos_ref[...], sin_ref[...]
    nq = GQA_HEADS * GQA_HEAD_DIM
    nkv = GQA_KV_HEADS * GQA_HEAD_DIM
    scale = GQA_HEAD_DIM ** -0.5
    for j in range(nq // 256):
        z = _dot(h, w_ref[:, j * 256:(j + 1) * 256]) + b_ref[:, j * 256:(j + 1) * 256]
        for c in range(2):
            r = _rope128(z[:, c * LANES:(c + 1) * LANES], cos, sin) * scale
            q_ref[:, j * 256 + c * LANES:j * 256 + (c + 1) * LANES] = r.astype(BF16)
    zk = _dot(h, w_ref[:, nq:nq + nkv]) + b_ref[:, nq:nq + nkv]
    zv = _dot(h, w_ref[:, nq + nkv:nq + 2 * nkv]) + b_ref[:, nq + nkv:nq + 2 * nkv]
    for c in range(nkv // LANES):
        r = _rope128(zk[:, c * LANES:(c + 1) * LANES], cos, sin)
        k_ref[2 * c] = r[:, :GQA_HEAD_DIM].astype(BF16)
        k_ref[2 * c + 1] = r[:, GQA_HEAD_DIM:].astype(BF16)
        v_ref[2 * c] = zv[:, c * LANES:c * LANES + GQA_HEAD_DIM].astype(BF16)
        v_ref[2 * c + 1] = zv[:, c * LANES + GQA_HEAD_DIM:(c + 1) * LANES].astype(BF16)


def _gqa_qkv(dims, x, g, mod, w, b, cos, sin, tm):
    T, D = dims.T, dims.D
    n = w.shape[1]
    nq = GQA_HEADS * GQA_HEAD_DIM
    kv_shape = jax.ShapeDtypeStruct((GQA_KV_HEADS, T, GQA_HEAD_DIM), BF16)
    kv_spec = pl.BlockSpec((GQA_KV_HEADS, tm, GQA_HEAD_DIM), lambda i: (0, i, 0))
    return pl.pallas_call(
        _gqa_qkv_body,
        grid=(T // tm,),
        in_specs=[_row_spec(tm, D), _const_spec((1, D)), _mod_spec(dims, tm, 0, D), _mod_spec(dims, tm, 1, D),
                  _const_spec((D, n)), _const_spec((1, n)), _row_spec(tm, LANES), _row_spec(tm, LANES)],
        out_specs=[_row_spec(tm, nq), kv_spec, kv_spec],
        out_shape=[jax.ShapeDtypeStruct((T, nq), BF16), kv_shape, kv_shape],
        compiler_params=_cparams("parallel"),
        name="gqa_qkv",
    )(x, g, mod, mod, w, b, cos, sin)


def _gqa_attn_body(sink_ref, q_ref, kp_ref, km_ref, kn_ref, kc_ref, vp_ref, vm_ref, vn_ref, vc_ref, o_ref,
                   *, tq, L, local):
    n = pl.program_id(1)
    kv = pl.program_id(2)
    kc, vc = kc_ref[0], vc_ref[0]
    if local:
        kl = jnp.concatenate([kp_ref[0], km_ref[0], kn_ref[0]], axis=0)
        vl = jnp.concatenate([vp_ref[0], vm_ref[0], vn_ref[0]], axis=0)
        nk = tq + 2 * WINDOW
        rel = lax.broadcasted_iota(I32, (tq, nk), 1) - lax.broadcasted_iota(I32, (tq, nk), 0)
        pos = n * tq - WINDOW + lax.broadcasted_iota(I32, (tq, nk), 1)
        valid = (rel >= 0) & (rel <= 2 * WINDOW) & (pos >= 0) & (pos < L)
    for gq in range(GQA_GROUP):
        qh = q_ref[:, gq * GQA_HEAD_DIM:(gq + 1) * GQA_HEAD_DIM]
        sk = sink_ref[kv * GQA_GROUP + gq]
        s_ctx = _dot_nt(qh, kc)
        m = jnp.maximum(jnp.max(s_ctx, axis=-1, keepdims=True), sk)
        if local:
            s_loc = jnp.where(valid, _dot_nt(qh, kl), -jnp.inf)
            m = jnp.maximum(m, jnp.max(s_loc, axis=-1, keepdims=True))
        p_ctx = jnp.exp(s_ctx - m)
        den = jnp.sum(p_ctx, axis=-1, keepdims=True) + jnp.exp(sk - m)
        o = _dot(p_ctx.astype(BF16), vc)
        if local:
            p_loc = jnp.exp(s_loc - m)
            den = den + jnp.sum(p_loc, axis=-1, keepdims=True)
            o = o + _dot(p_loc.astype(BF16), vl)
        o_ref[:, gq * GQA_HEAD_DIM:(gq + 1) * GQA_HEAD_DIM] = (o / den).astype(BF16)


def _gqa_attn(dims, q, k, v, sink, *, local, tq):
    B, L, C, T = dims.B, dims.L, dims.C, dims.T
    gw = GQA_GROUP * GQA_HEAD_DIM
    hb = WINDOW
    if local:
        nq, row0, n_out = L // tq, lambda b, n: b * (L // tq) + n, B * L
    else:
        nq, row0, n_out = 1, lambda b, n: (B * L) // tq + b, B * C
    out_row0 = (lambda b, n: row0(b, n)) if local else (lambda b, n: b)
    last_hb = T // hb - 1

    def q_map(b, n, kv, s):
        return (row0(b, n), kv)

    def o_map(b, n, kv, s):
        return (out_row0(b, n), kv)

    def main_map(b, n, kv, s):
        return (kv, row0(b, n), 0)

    def prev_map(b, n, kv, s):
        return (kv, jnp.maximum(row0(b, n) * (tq // hb) - 1, 0), 0)

    def next_map(b, n, kv, s):
        return (kv, jnp.minimum((row0(b, n) + 1) * (tq // hb), last_hb), 0)

    def ctx_map(b, n, kv, s):
        return (kv, (B * L) // C + b, 0)

    dh = GQA_HEAD_DIM
    kv_specs = [pl.BlockSpec((1, hb, dh), prev_map), pl.BlockSpec((1, tq, dh), main_map),
                pl.BlockSpec((1, hb, dh), next_map), pl.BlockSpec((1, C, dh), ctx_map)]
    grid_spec = pltpu.PrefetchScalarGridSpec(
        num_scalar_prefetch=1,
        grid=(B, nq, GQA_KV_HEADS),
        in_specs=[pl.BlockSpec((tq, gw), q_map)] + kv_specs + kv_specs,
        out_specs=pl.BlockSpec((tq, gw), o_map),
    )
    return pl.pallas_call(
        functools.partial(_gqa_attn_body, tq=tq, L=L, local=local),
        grid_spec=grid_spec, out_shape=jax.ShapeDtypeStruct((n_out, q.shape[1]), BF16),
        compiler_params=_cparams("parallel", "parallel", "parallel"),
        name="gqa_attn_local" if local else "gqa_attn_ctx",
    )(sink, q, k, k, k, k, v, v, v, v)


def _proj_res_body(*refs, tn, n_lat_tiles, with_ctx):
    if with_ctx:
        al_ref, ac_ref, w_ref, b_ref, x_ref, gate_ref, o_ref = refs
    else:
        al_ref, w_ref, b_ref, x_ref, gate_ref, o_ref = refs
    gate = gate_ref[0]

    def run(a_ref):
        a = a_ref[...]
        for j in range(o_ref.shape[1] // tn):
            sl = slice(j * tn, (j + 1) * tn)
            y = _dot(a, w_ref[:, sl]) + b_ref[:, sl]
            o_ref[:, sl] = x_ref[:, sl] + gate[:, sl] * y

    if with_ctx:
        is_lat = pl.program_id(0) < n_lat_tiles
        pl.when(is_lat)(lambda: run(al_ref))
        pl.when(jnp.logical_not(is_lat))(lambda: run(ac_ref))
    else:
        run(al_ref)


def _proj_res(dims, a_lat, a_ctx, w, b, x, mod, chunk, tm):
    T, D = dims.T, dims.D
    K = a_lat.shape[1]
    n_lat_tiles = dims.n_lat // tm
    with_ctx = a_ctx is not None
    a_specs = [pl.BlockSpec((tm, K), lambda i: (jnp.minimum(i, n_lat_tiles - 1), 0))]
    args = [a_lat]
    if with_ctx:
        a_specs.append(pl.BlockSpec((tm, K), lambda i: (jnp.maximum(i - n_lat_tiles, 0), 0)))
        args.append(a_ctx)
    return pl.pallas_call(
        functools.partial(_proj_res_body, tn=512, n_lat_tiles=n_lat_tiles, with_ctx=with_ctx),
        grid=(T // tm if with_ctx else n_lat_tiles,),
        in_specs=a_specs + [_const_spec((K, D)), _const_spec((1, D)), _row_spec(tm, D),
                            _mod_spec(dims, tm, chunk, D)],
        out_specs=_row_spec(tm, D),
        out_shape=jax.ShapeDtypeStruct((T, D), F32),
        input_output_aliases={len(args) + 2: 0},
        compiler_params=_cparams("parallel"),
        name="proj_residual",
    )(*args, w, b, x, mod)


def _pool_body(xp_ref, x_ref, xn_ref, g_ref, sh_ref, sc_ref, gate_ref, w_ref, b_ref, ps_ref, o_ref, hf_ref,
               *, tm, dims):
    i = pl.program_id(0)
    row0 = i * tm
    is_lat = row0 < dims.n_lat
    n = jnp.where(is_lat, dims.L, dims.C)
    pos0 = jnp.where(is_lat, row0 % dims.L, (row0 - dims.n_lat) % dims.C)
    g, sh, sc = g_ref[...], sh_ref[0], sc_ref[0]
    x = x_ref[...]
    h = _norm_mod(x, g, sh, sc)
    hp = _norm_mod(xp_ref[...], g, sh, sc)
    hn = _norm_mod(xn_ref[...], g, sh, sc)
    hf_ref[0:POOL_HALO, :] = jnp.where(pos0 > 0, hp, 0.0)
    hf_ref[POOL_HALO:POOL_HALO + tm, :] = h
    hf_ref[POOL_HALO + tm:, :] = jnp.where(pos0 + tm < n, hn, 0.0)
    pos = pos0 + lax.broadcasted_iota(I32, (tm, 1), 0)
    gate = gate_ref[0]
    gw = dims.D // len(POOL_WINDOWS)
    for gi, win in enumerate(POOL_WINDOWS):
        half = win // 2
        sl = slice(gi * gw, (gi + 1) * gw)
        acc = hf_ref[POOL_HALO - half:POOL_HALO - half + tm, sl]
        for o in range(-half + 1, half):
            acc = acc + hf_ref[POOL_HALO + o:POOL_HALO + o + tm, sl]
        cnt = (jnp.minimum(pos + half, n) - jnp.maximum(pos - half, 0)).astype(F32)
        d = (acc / cnt - h[:, sl]).astype(BF16)
        y = (_dot(d, w_ref[gi]) + b_ref[:, sl]) * ps_ref[:, sl]
        o_ref[:, sl] = x[:, sl] + gate[:, sl] * y


def _pool(dims, x, g, mod, w, b, ps, tm):
    T, D = dims.T, dims.D
    hb = tm // POOL_HALO
    last = T // POOL_HALO - 1
    gw = D // len(POOL_WINDOWS)
    return pl.pallas_call(
        functools.partial(_pool_body, tm=tm, dims=dims),
        grid=(T // tm,),
        in_specs=[pl.BlockSpec((POOL_HALO, D), lambda i: (jnp.maximum(i * hb - 1, 0), 0)),
                  _row_spec(tm, D),
                  pl.BlockSpec((POOL_HALO, D), lambda i: (jnp.minimum((i + 1) * hb, last), 0)),
                  _const_spec((1, D)), _mod_spec(dims, tm, 0, D), _mod_spec(dims, tm, 1, D),
                  _mod_spec(dims, tm, 2, D), _const_spec((len(POOL_WINDOWS), gw, gw)),
                  _const_spec((1, D)), _const_spec((1, D))],
        out_specs=_row_spec(tm, D),
        out_shape=jax.ShapeDtypeStruct((T, D), F32),
        scratch_shapes=[pltpu.VMEM((tm + 2 * POOL_HALO, D), F32)],
        compiler_params=_cparams("parallel"),
        name="pool_mixer",
    )(x, x, x, g, mod, mod, mod, w, b, ps)


def _mla_down_body(x_ref, g_ref, sh_ref, sc_ref, w_ref, gq_ref, gkv_ref, cos_ref, sin_ref,
                   cq_ref, ckv_ref, kr_ref):
    h = _norm_mod(x_ref[...], g_ref[...], sh_ref[0], sc_ref[0]).astype(BF16)

    def rms(z, gain):
        return (z * lax.rsqrt(jnp.mean(z * z, axis=-1, keepdims=True) + EPS) * gain).astype(BF16)

    cq_ref[...] = rms(_dot(h, w_ref[:, :MLA_Q_RANK]), gq_ref[...])
    ckv_ref[...] = rms(_dot(h, w_ref[:, MLA_Q_RANK:MLA_Q_RANK + MLA_KV_RANK]), gkv_ref[...])
    zr = _dot(h, w_ref[:, MLA_Q_RANK + MLA_KV_RANK:])
    kr_ref[...] = _rope128(zr, cos_ref[...], sin_ref[...])[:, :MLA_ROPE].astype(BF16)


def _mla_down(dims, x, g, mod, w_pad, gq, gkv, cos, sin, tm):
    T, D = dims.T, dims.D
    n = w_pad.shape[1]
    return pl.pallas_call(
        _mla_down_body,
        grid=(T // tm,),
        in_specs=[_row_spec(tm, D), _const_spec((1, D)), _mod_spec(dims, tm, 0, D), _mod_spec(dims, tm, 1, D),
                  _const_spec((D, n)), _const_spec((1, MLA_Q_RANK)), _const_spec((1, MLA_KV_RANK)),
                  _row_spec(tm, LANES), _row_spec(tm, LANES)],
        out_specs=[_row_spec(tm, MLA_Q_RANK), _row_spec(tm, MLA_KV_RANK), _row_spec(tm, MLA_ROPE)],
        out_shape=[jax.ShapeDtypeStruct((T, MLA_Q_RANK), BF16), jax.ShapeDtypeStruct((T, MLA_KV_RANK), BF16),
                   jax.ShapeDtypeStruct((T, MLA_ROPE), BF16)],
        compiler_params=_cparams("parallel"),
        name="mla_down",
    )(x, g, mod, mod, w_pad, gq, gkv, cos, sin)


def _mla_up_body(cq_ref, ckv_ref, kr_ref, cos_ref, sin_ref, wqn_ref, wqr_ref, wk_ref, wv_ref,
                 q_ref, k_ref, v_ref):
    cq, ckv, kr = cq_ref[...], ckv_ref[...], kr_ref[...]
    cos, sin = cos_ref[...], sin_ref[...]
    scale = MLA_QK ** -0.5
    H = MLA_HEADS
    for j in range(H // 2):
        zq = _dot(cq, wqn_ref[:, j * 256:(j + 1) * 256]) * scale
        zk = _dot(ckv, wk_ref[:, j * 256:(j + 1) * 256])
        zv = _dot(ckv, wv_ref[:, j * 256:(j + 1) * 256])
        for c in range(2):
            hh = 2 * j + c
            q_ref[hh, :, :MLA_NOPE] = zq[:, c * LANES:(c + 1) * LANES].astype(BF16)
            k_ref[hh, :, :MLA_NOPE] = zk[:, c * LANES:(c + 1) * LANES].astype(BF16)
            k_ref[hh, :, MLA_NOPE:] = kr
            v_ref[hh] = zv[:, c * LANES:(c + 1) * LANES].astype(BF16)
    for j in range(H // 4):
        zr = _dot(cq, wqr_ref[:, j * 256:(j + 1) * 256])
        for c in range(2):
            r = (_rope128(zr[:, c * LANES:(c + 1) * LANES], cos, sin) * scale).astype(BF16)
            q_ref[4 * j + 2 * c, :, MLA_NOPE:] = r[:, :MLA_ROPE]
            q_ref[4 * j + 2 * c + 1, :, MLA_NOPE:] = r[:, MLA_ROPE:]


def _mla_up(dims, cq, ckv, kr, cos, sin, wqn, wqr, wk, wv, tm):
    T = dims.T
    H = MLA_HEADS

    def hspec(n):
        return pl.BlockSpec((H, tm, n), lambda i: (0, i, 0))

    return pl.pallas_call(
        _mla_up_body,
        grid=(T // tm,),
        in_specs=[_row_spec(tm, MLA_Q_RANK), _row_spec(tm, MLA_KV_RANK), _row_spec(tm, MLA_ROPE),
                  _row_spec(tm, LANES), _row_spec(tm, LANES),
                  _const_spec(wqn.shape), _const_spec(wqr.shape), _const_spec(wk.shape), _const_spec(wv.shape)],
        out_specs=[hspec(MLA_QK), hspec(MLA_QK), hspec(MLA_V)],
        out_shape=[jax.ShapeDtypeStruct((H, T, MLA_QK), BF16), jax.ShapeDtypeStruct((H, T, MLA_QK), BF16),
                   jax.ShapeDtypeStruct((H, T, MLA_V), BF16)],
        compiler_params=_cparams("parallel"),
        name="mla_up",
    )(cq, ckv, kr, cos, sin, wqn, wqr, wk, wv)


def _mla_attn_body(*refs, tk, n_chunks, latent_keys):
    if latent_keys:
        q_ref, kc_ref, vc_ref, kl_ref, vl_ref, o_ref, acc_ref = refs
    else:
        q_ref, kc_ref, vc_ref, o_ref, acc_ref = refs
    q = q_ref[0]
    s = _dot_nt(q, kc_ref[0])
    m = jnp.max(s, axis=-1, keepdims=True)
    p = jnp.exp(s - m)
    den = jnp.sum(p, axis=-1, keepdims=True)
    acc_ref[...] = _dot(p.astype(BF16), vc_ref[0])
    if latent_keys:
        def step(c, carry):
            m, den = carry
            off = pl.multiple_of(c * tk, tk)
            s = _dot_nt(q, kl_ref[0, pl.ds(off, tk), :])
            m2 = jnp.maximum(m, jnp.max(s, axis=-1, keepdims=True))
            a = jnp.exp(m - m2)
            p = jnp.exp(s - m2)
            acc_ref[...] = a * acc_ref[...] + _dot(p.astype(BF16), vl_ref[0, pl.ds(off, tk), :])
            return m2, a * den + jnp.sum(p, axis=-1, keepdims=True)

        m, den = lax.fori_loop(0, n_chunks, step, (m, den))
    o_ref[...] = (acc_ref[...] / den).astype(BF16)


def _mla_attn(dims, q, k, v, *, latent_queries, tq, tk):
    B, L, C = dims.B, dims.L, dims.C
    H = MLA_HEADS
    if latent_queries:
        nq, row0, n_out = L // tq, lambda b, n: b * (L // tq) + n, B * L
    else:
        nq, row0, n_out = 1, lambda b, n: (B * L) // tq + b, B * C
    out_row0 = row0 if latent_queries else (lambda b, n: b)

    def q_map(b, h, n):
        return (h, row0(b, n), 0)

    def ctx_map(b, h, n):
        return (h, (B * L) // C + b, 0)

    def lat_map(b, h, n):
        return (h, b, 0)

    def o_map(b, h, n):
        return (out_row0(b, n), h)

    in_specs = [pl.BlockSpec((1, tq, MLA_QK), q_map),
                pl.BlockSpec((1, C, MLA_QK), ctx_map), pl.BlockSpec((1, C, MLA_V), ctx_map)]
    args = [q, k, v]
    if latent_queries:
        in_specs += [pl.BlockSpec((1, L, MLA_QK), lat_map), pl.BlockSpec((1, L, MLA_V), lat_map)]
        args += [k, v]
    return pl.pallas_call(
        functools.partial(_mla_attn_body, tk=tk, n_chunks=L // tk, latent_keys=latent_queries),
        grid=(B, H, nq),
        in_specs=in_specs,
        out_specs=pl.BlockSpec((tq, MLA_V), o_map),
        out_shape=jax.ShapeDtypeStruct((n_out, H * MLA_V), BF16),
        scratch_shapes=[pltpu.VMEM((tq, MLA_V), F32)],
        compiler_params=_cparams("parallel", "parallel", "arbitrary"),
        name="mla_attn_latent" if latent_queries else "mla_attn_ctx",
    )(*args)


def _router_body(x_ref, g_ref, sh_ref, sc_ref, whi_ref, wlo_ref, br_ref, hn_ref, rho_ref, gate_ref, cnt_ref):
    h = _norm_mod(x_ref[...], g_ref[...], sh_ref[0], sc_ref[0])
    hi = h.astype(BF16)
    lo = (h - hi.astype(F32)).astype(BF16)
    hn_ref[...] = hi
    whi, wlo = whi_ref[...], wlo_ref[...]
    lt = _dot_nt(whi, hi) + _dot_nt(wlo, hi) + _dot_nt(whi, lo) + br_ref[...]
    E = N_EXPERTS
    eidx = lax.broadcasted_iota(I32, (E, SUB), 0)
    onehots, tops = [], []
    for _ in range(TOP_K):
        m = jnp.max(lt, axis=0, keepdims=True)
        idx = jnp.min(jnp.where(lt == m, eidx, E), axis=0, keepdims=True)
        oh = eidx == idx
        onehots.append(oh)
        tops.append(m)
        lt = jnp.where(oh, -jnp.inf, lt)
    ex = [jnp.exp(t - tops[0]) for t in tops]
    den = ex[0] + ex[1] + ex[2] + ex[3]
    chosen = (onehots[0] | onehots[1] | onehots[2] | onehots[3])
    chosen_f = jnp.where(chosen, 1.0, 0.0)
    cnt = jnp.sum(chosen_f, axis=1, keepdims=True)
    units = jnp.floor((cnt + (UNIT - 1)) / UNIT)
    cnt_ref[0] = units
    lower = (lax.broadcasted_iota(I32, (E, E), 1) < lax.broadcasted_iota(I32, (E, E), 0))
    start = _dot(jnp.where(lower, 1.0, 0.0).astype(BF16),
                 jnp.broadcast_to(units, (E, LANES)).astype(BF16))[:, :1] * UNIT
    before = (lax.broadcasted_iota(I32, (SUB, SUB), 0) < lax.broadcasted_iota(I32, (SUB, SUB), 1))
    rank = _dot(chosen_f.astype(BF16), jnp.where(before, 1.0, 0.0).astype(BF16))
    row = start + rank
    for k in range(TOP_K):
        rho_ref[k:k + 1, :] = jnp.sum(jnp.where(onehots[k], row, 0.0), axis=0, keepdims=True).astype(I32)
        gate_ref[k:k + 1, :] = ex[k] / den


def _router(dims, x, g, mod, whi, wlo, br):
    T, D = dims.T, dims.D
    n_sub = T // SUB
    E = N_EXPERTS
    return pl.pallas_call(
        _router_body,
        grid=(n_sub,),
        in_specs=[_row_spec(SUB, D), _const_spec((1, D)), _mod_spec(dims, SUB, 3, D), _mod_spec(dims, SUB, 4, D),
                  _const_spec((E, D)), _const_spec((E, D)), _const_spec((E, 1))],
        out_specs=[_row_spec(SUB, D), pl.BlockSpec((TOP_K, SUB), lambda i: (0, i)),
                   pl.BlockSpec((TOP_K, SUB), lambda i: (0, i)), pl.BlockSpec((1, E, 1), lambda i: (i, 0, 0))],
        out_shape=[jax.ShapeDtypeStruct((T, D), BF16), jax.ShapeDtypeStruct((TOP_K, T), I32),
                   jax.ShapeDtypeStruct((TOP_K, T), F32), jax.ShapeDtypeStruct((n_sub, E, 1), F32)],
        compiler_params=_cparams("parallel"),
        name="moe_router",
    )(x, g, mod, mod, whi, wlo, br)


def _dispatch_body(hn_ref, rho_ref, a_ref, *, n_sub):
    u = pl.program_id(0)
    rows = lax.broadcasted_iota(I32, (SUB_ROWS, SUB), 0)
    hit = rows == rho_ref[0:1, :]
    for k in range(1, TOP_K):
        hit = hit | (rows == rho_ref[k:k + 1, :])
    sel = jnp.where(hit & (u < n_sub), 1.0, 0.0).astype(BF16)
    a_ref[...] = _dot(sel, hn_ref[...]).astype(BF16)


def _dispatch(dims, hn, rho):
    T, D = dims.T, dims.D
    n_sub = T // SUB
    last = n_sub - 1
    return pl.pallas_call(
        functools.partial(_dispatch_body, n_sub=n_sub),
        grid=(n_sub + 1,),
        in_specs=[pl.BlockSpec((SUB, D), lambda u: (jnp.minimum(u, last), 0)),
                  pl.BlockSpec((TOP_K, SUB), lambda u: (0, jnp.minimum(u, last)))],
        out_specs=pl.BlockSpec((SUB_ROWS, D), lambda u: (u, 0)),
        out_shape=jax.ShapeDtypeStruct(((n_sub + 1) * SUB_ROWS, D), BF16),
        compiler_params=_cparams("parallel"),
        name="moe_dispatch",
    )(hn, rho)


def _expert_body(src_ref, be_ref, nu_ref, *refs):
    xs = refs[:BLOCK_UNITS]
    w1_ref, b1_ref, w2_ref, b2_ref, y_ref = refs[BLOCK_UNITS:]
    j = pl.program_id(0)

    @pl.when(j < nu_ref[0])
    def _():
        x = jnp.concatenate([r[...] for r in xs], axis=0)
        a = _dot(x, w1_ref[0]) + b1_ref[0]
        a_glu = jnp.minimum(a[:, :EXPERT_FF], SWIGLU_LIMIT)
        a_lin = jnp.clip(a[:, EXPERT_FF:], -SWIGLU_LIMIT, SWIGLU_LIMIT)
        u = a_glu * jax.nn.sigmoid(SWIGLU_ALPHA * a_glu) * (a_lin + 1.0)
        y_ref[...] = (_dot(u.astype(BF16), w2_ref[0]) + b2_ref[0]).astype(BF16)

    @pl.when(j >= nu_ref[0])
    def _():
        y_ref[...] = jnp.zeros_like(y_ref)


def _experts(a, src_unit, blk_e, n_used, w1, b1, w2, b2, n_blocks):
    D = a.shape[1]
    ff2 = w1.shape[2]

    def unit_map(i, j, src, be, nu):
        return (src[j * BLOCK_UNITS + i], 0)

    def e_map(j, src, be, nu):
        return (be[j], 0, 0)

    grid_spec = pltpu.PrefetchScalarGridSpec(
        num_scalar_prefetch=3,
        grid=(n_blocks,),
        in_specs=[pl.BlockSpec((UNIT, D), functools.partial(unit_map, i)) for i in range(BLOCK_UNITS)]
        + [pl.BlockSpec((1, D, ff2), e_map), pl.BlockSpec((1, 1, ff2), e_map),
           pl.BlockSpec((1, EXPERT_FF, D), e_map), pl.BlockSpec((1, 1, D), e_map)],
        out_specs=pl.BlockSpec((EXPERT_BLOCK, D), lambda j, src, be, nu: (j, 0)),
    )
    return pl.pallas_call(
        _expert_body,
        grid_spec=grid_spec,
        out_shape=jax.ShapeDtypeStruct((n_blocks * EXPERT_BLOCK, D), BF16),
        compiler_params=_cparams("arbitrary"),
        name="moe_experts",
    )(src_unit, blk_e, n_used, *([a] * BLOCK_UNITS), w1, b1, w2, b2)


def _combine_body(dst_ref, x_ref, rho_ref, gate_ref, g2_ref, ys_ref, o_ref, buf_ref, sem_ref, *, n_sub):
    u = pl.program_id(0)

    def unit_copy(uu, slot, i):
        d = dst_ref[uu * SUB_UNITS + i]
        return d, pltpu.make_async_copy(ys_ref.at[pl.ds(pl.multiple_of(jnp.maximum(d, 0) * UNIT, UNIT), UNIT)],
                                        buf_ref.at[slot, pl.ds(pl.multiple_of(i * UNIT, UNIT), UNIT)],
                                        sem_ref.at[slot])

    def start_all(uu, slot):
        def f(i, c):
            d, cp = unit_copy(uu, slot, i)

            @pl.when(d >= 0)
            def _():
                cp.start()
            return c
        lax.fori_loop(0, SUB_UNITS, f, 0)

    def wait_all(uu, slot):
        def f(i, c):
            d, cp = unit_copy(uu, slot, i)

            @pl.when(d >= 0)
            def _():
                cp.wait()
            return c
        lax.fori_loop(0, SUB_UNITS, f, 0)

    @pl.when(u == 0)
    def _():
        buf_ref[...] = jnp.zeros_like(buf_ref)
        start_all(0, 0)

    @pl.when(u + 1 < n_sub)
    def _():
        start_all(u + 1, (u + 1) % 2)

    slot = u % 2
    wait_all(u, slot)
    cols = lax.broadcasted_iota(I32, (SUB, SUB_ROWS), 1)
    wgt = jnp.where(cols == rho_ref[:, 0:1], gate_ref[:, 0:1], 0.0)
    for k in range(1, TOP_K):
        wgt = wgt + jnp.where(cols == rho_ref[:, k:k + 1], gate_ref[:, k:k + 1], 0.0)
    y = _dot(wgt.astype(BF16), buf_ref[slot])
    o_ref[...] = x_ref[...] + g2_ref[0] * y


def _combine(dims, dst_unit, x, rho_t, gate_t, mod, ys):
    T, D = dims.T, dims.D
    n_sub = T // SUB
    grid_spec = pltpu.PrefetchScalarGridSpec(
        num_scalar_prefetch=1,
        grid=(n_sub,),
        in_specs=[pl.BlockSpec((SUB, D), lambda u, dst: (u, 0)),
                  pl.BlockSpec((SUB, TOP_K), lambda u, dst: (u, 0)),
                  pl.BlockSpec((SUB, TOP_K), lambda u, dst: (u, 0)),
                  pl.BlockSpec((1, 1, D), lambda u, dst: (dims.mod_row(SUB)(u) * 6 + 5, 0, 0)),
                  pl.BlockSpec(memory_space=pl.ANY)],
        out_specs=pl.BlockSpec((SUB, D), lambda u, dst: (u, 0)),
        scratch_shapes=[pltpu.VMEM((2, SUB_ROWS, D), BF16), pltpu.SemaphoreType.DMA((2,))],
    )
    return pl.pallas_call(
        functools.partial(_combine_body, n_sub=n_sub),
        grid_spec=grid_spec,
        out_shape=jax.ShapeDtypeStruct((T, D), F32),
        input_output_aliases={1: 0},
        compiler_params=_cparams("arbitrary"),
        name="moe_combine",
    )(dst_unit, x, rho_t, gate_t, mod, ys)


def _moe_tables(units, n_blocks):
    n_sub, E = units.shape
    g = units.astype(I32)
    off_a = jnp.cumsum(g, axis=1) - g
    tot = jnp.sum(g, axis=0)
    pad = (tot + BLOCK_UNITS - 1) // BLOCK_UNITS * BLOCK_UNITS
    pend = jnp.cumsum(pad)
    pos_x = (pend - pad)[None, :] + jnp.cumsum(g, axis=0) - g
    starts = pos_x.T.reshape(-1)
    gflat = g.T.reshape(-1)
    p = jnp.arange(n_blocks * BLOCK_UNITS, dtype=I32)
    idx = jnp.clip(jnp.searchsorted(starts, p, side="right") - 1, 0, E * n_sub - 1).astype(I32)
    r = p - starts[idx]
    src = (idx % n_sub) * SUB_UNITS + off_a.T.reshape(-1)[idx] + r
    src_unit = jnp.where(r < gflat[idx], src, n_sub * SUB_UNITS).astype(I32)
    blk_e = jnp.minimum(jnp.searchsorted(pend, jnp.arange(n_blocks, dtype=I32) * BLOCK_UNITS, side="right"),
                        E - 1).astype(I32)
    n_used = (pend[-1:] // BLOCK_UNITS).astype(I32)
    i = jnp.arange(SUB_UNITS, dtype=I32)
    ends = jnp.cumsum(g, axis=1)
    e_of = jnp.sum(i[None, :, None] >= ends[:, None, :], axis=-1).astype(I32)
    e_cl = jnp.minimum(e_of, E - 1)
    dst = jnp.take_along_axis(pos_x, e_cl, axis=1) + i[None, :] - jnp.take_along_axis(off_a, e_cl, axis=1)
    dst_unit = jnp.where(e_of < E, dst, -1).astype(I32).reshape(-1)
    return src_unit, blk_e, n_used, dst_unit


def _moe(dims, x, g, mod, whi, wlo, br, w1, b1, w2, b2):
    T = dims.T
    n_sub = T // SUB
    n_blocks = (n_sub * SUB_UNITS + N_EXPERTS * (BLOCK_UNITS - 1)) // BLOCK_UNITS + 1
    hn, rho, gate, units = _router(dims, x, g, mod, whi, wlo, br)
    src_unit, blk_e, n_used, dst_unit = _moe_tables(units.reshape(n_sub, N_EXPERTS), n_blocks)
    a = _dispatch(dims, hn, rho)
    ys = _experts(a, src_unit, blk_e, n_used, w1, b1, w2, b2, n_blocks)
    return _combine(dims, dst_unit, x, rho.T, gate.T, mod, ys)


def _final_body(x_ref, g_ref, o_ref):
    x = x_ref[...]
    o_ref[...] = x * lax.rsqrt(jnp.mean(x * x, axis=-1, keepdims=True) + EPS) * g_ref[...]


def _final_norm(x, g, n_rows, tm):
    D = x.shape[1]
    return pl.pallas_call(
        _final_body,
        grid=(n_rows // tm,),
        in_specs=[_row_spec(tm, D), _const_spec((1, D))],
        out_specs=_row_spec(tm, D),
        out_shape=jax.ShapeDtypeStruct((n_rows, D), F32),
        compiler_params=_cparams("parallel"),
        name="final_norm",
    )(x, g)


def _rope_tables(B, L, C):
    half = GQA_HEAD_DIM // 4
    inv = ROPE_BASE ** (-jnp.arange(half, dtype=F32) / half)
    t = jnp.arange(L)
    ang_r = (t // GRID_W).astype(F32)[:, None] * inv[None, :]
    ang_c = (t % GRID_W).astype(F32)[:, None] * inv[None, :]
    cos = jnp.concatenate([jnp.cos(ang_r)] * 2 + [jnp.cos(ang_c)] * 2, axis=-1)
    sin = jnp.concatenate([-jnp.sin(ang_r), jnp.sin(ang_r), -jnp.sin(ang_c), jnp.sin(ang_c)], axis=-1)
    cos = jnp.tile(cos, (B, LANES // cos.shape[1]))
    sin = jnp.tile(sin, (B, LANES // sin.shape[1]))
    cos = jnp.concatenate([cos, jnp.ones((B * C, LANES), F32)], axis=0)
    sin = jnp.concatenate([sin, jnp.zeros((B * C, LANES), F32)], axis=0)
    return cos, sin


def kernel(x, c, ctx, c_ctx, ada_w, ada_b, norm_g, final_g, gqa_w_qkv, gqa_b_qkv, gqa_sink, gqa_w_o, gqa_b_o,
           pool_w, pool_b, pool_scale, mla_w_down, mla_g_q, mla_w_uq, mla_g_kv, mla_w_ukv, mla_w_o,
           router_w, router_b, exp_w1, exp_b1, exp_w2, exp_b2):
    B, L, D = x.shape
    C = ctx.shape[1]
    depth = ada_w.shape[0]
    dims = _Dims(B, L, C, D)
    T = dims.T
    tm = 512 if (L % 512 == 0 and (B * C) % 512 == 0) else 256
    assert L % tm == 0 and (B * C) % tm == 0 and L % SUB == 0 and C % SUB == 0 and B + 1 <= UNIT
    assert L % GRID_W == 0 and SUB % WINDOW == 0 and C % WINDOW == 0

    cc = jnp.zeros((UNIT, D), F32).at[:B].set(c).at[B].set(c_ctx)
    mods = _ada_mod(cc, ada_w, ada_b)
    cos, sin = _rope_tables(B, L, C)
    xs = jnp.concatenate([x.reshape(B * L, D), ctx.reshape(B * C, D)], axis=0)

    for i in range(depth):
        need_ctx = i < depth - 1
        mod = mods[i].reshape(UNIT * 6, 1, D)
        kind, j = i % N_MIXERS, i // N_MIXERS
        g1 = norm_g[i, 0].reshape(1, D)
        if kind == 0:
            q, k, v = _gqa_qkv(dims, xs, g1, mod, gqa_w_qkv[j].astype(BF16), gqa_b_qkv[j].reshape(1, -1),
                               cos, sin, tm)
            o = _gqa_attn(dims, q, k, v, gqa_sink[j], local=True, tq=SUB)
            oc = _gqa_attn(dims, q, k, v, gqa_sink[j], local=False, tq=C) if need_ctx else None
            xs = _proj_res(dims, o, oc, gqa_w_o[j].astype(BF16), gqa_b_o[j].reshape(1, D), xs, mod, 2, tm)
        elif kind == 1:
            xs = _pool(dims, xs, g1, mod, pool_w[j].astype(BF16), pool_b[j].reshape(1, D),
                       pool_scale[j].reshape(1, D), SUB)
        else:
            H = MLA_HEADS
            w_down = jnp.pad(mla_w_down[j], ((0, 0), (0, LANES - MLA_ROPE))).astype(BF16)
            wq = mla_w_uq[j].reshape(MLA_Q_RANK, H, MLA_QK)
            wqn = wq[:, :, :MLA_NOPE].reshape(MLA_Q_RANK, H * MLA_NOPE).astype(BF16)
            wqr = wq[:, :, MLA_NOPE:].reshape(MLA_Q_RANK, H * MLA_ROPE).astype(BF16)
            wkv = mla_w_ukv[j].reshape(MLA_KV_RANK, H, MLA_NOPE + MLA_V)
            wk = wkv[:, :, :MLA_NOPE].reshape(MLA_KV_RANK, H * MLA_NOPE).astype(BF16)
            wv = wkv[:, :, MLA_NOPE:].reshape(MLA_KV_RANK, H * MLA_V).astype(BF16)
            cq, ckv, kr = _mla_down(dims, xs, g1, mod, w_down, mla_g_q[j].reshape(1, -1),
                                    mla_g_kv[j].reshape(1, -1), cos, sin, tm)
            q, k, v = _mla_up(dims, cq, ckv, kr, cos, sin, wqn, wqr, wk, wv, SUB)
            o = _mla_attn(dims, q, k, v, latent_queries=True, tq=tm, tk=min(512, L))
            oc = _mla_attn(dims, q, k, v, latent_queries=False, tq=C, tk=min(512, L)) if need_ctx else None
            xs = _proj_res(dims, o, oc, mla_w_o[j].astype(BF16), jnp.zeros((1, D), F32), xs, mod, 2, tm)
        wr_t = router_w[i].T
        whi = wr_t.astype(BF16)
        wlo = (wr_t - whi.astype(F32)).astype(BF16)
        xs = _moe(dims, xs, norm_g[i, 1].reshape(1, D), mod, whi, wlo, router_b[i].reshape(-1, 1),
                  exp_w1[i].astype(BF16), exp_b1[i][:, None, :], exp_w2[i].astype(BF16), exp_b2[i][:, None, :])
    out = _final_norm(xs, final_g.reshape(1, D), B * L, tm)
    return out.reshape(B, L, D)
```

```python
import functools

import jax
import jax.numpy as jnp
from jax import lax
from jax.experimental import pallas as pl
from jax.experimental.pallas import tpu as pltpu

F32 = jnp.float32
BF16 = jnp.bfloat16
I32 = jnp.int32

GRID_W = 64
N_MIXERS = 3
EPS = 1e-6
ROPE_BASE = 10000.0

GQA_HEADS = 32
GQA_KV_HEADS = 4
GQA_HEAD_DIM = 64
GQA_GROUP = GQA_HEADS // GQA_KV_HEADS
WINDOW = 128
GQA_LOOKAHEAD = 2
GQA_VT_ROWS = GQA_HEAD_DIM + 16

POOL_WINDOWS = (2, 4, 8, 16)
POOL_HALO = 8

MLA_HEADS = 16
MLA_Q_RANK = 512
MLA_KV_RANK = 512
MLA_NOPE = 128
MLA_ROPE = 64
MLA_V = 128
MLA_QK = MLA_NOPE + MLA_ROPE
MLA_VT_ROWS = MLA_V + 16
MLA_LOOKAHEAD = 3

N_EXPERTS = 32
TOP_K = 4
EXPERT_FF = 768
SWIGLU_LIMIT = 7.0
SWIGLU_ALPHA = 1.702

LANES = 128
UNIT = 16
SUB = 256
SUB_UNITS = (SUB * TOP_K + N_EXPERTS * (UNIT - 1) + UNIT - 1) // UNIT
SUB_ROWS = SUB_UNITS * UNIT
EXPERT_BLOCK = 256
BLOCK_UNITS = EXPERT_BLOCK // UNIT
VMEM_LIMIT = 56 * 1024 * 1024


def _cparams(*sem):
    return pltpu.CompilerParams(dimension_semantics=sem, vmem_limit_bytes=VMEM_LIMIT)


def _dot(a, b):
    return jnp.dot(a, b, preferred_element_type=F32)


def _dot_nt(a, b):
    return lax.dot_general(a, b, (((1,), (1,)), ((), ())), preferred_element_type=F32)


def _norm_mod(x, g, shift, scale):
    y = x * lax.rsqrt(jnp.mean(x * x, axis=-1, keepdims=True) + EPS) * g
    return y * (1.0 + scale) + shift


def _rope128(z, cos, sin):
    lane = lax.broadcasted_iota(I32, (1, LANES), 1)
    partner = jnp.where((lane % 32) < 16, pltpu.roll(z, LANES - 16, 1), pltpu.roll(z, 16, 1))
    return z * cos + partner * sin


def _ada_body(c_ref, w_ref, b_ref, o_ref):
    c = c_ref[...]
    s = (c * jax.nn.sigmoid(c)).astype(BF16)
    o_ref[0] = _dot(s, w_ref[0].astype(BF16)) + b_ref[0]


def _ada_mod(cc, ada_w, ada_b):
    depth, d, n = ada_w.shape
    tn = 1024
    return pl.pallas_call(
        _ada_body,
        grid=(depth, n // tn),
        in_specs=[pl.BlockSpec((UNIT, d), lambda i, j: (0, 0)),
                  pl.BlockSpec((1, d, tn), lambda i, j: (i, 0, j)),
                  pl.BlockSpec((1, 1, tn), lambda i, j: (i, 0, j))],
        out_specs=pl.BlockSpec((1, UNIT, tn), lambda i, j: (i, 0, j)),
        out_shape=jax.ShapeDtypeStruct((depth, UNIT, n), F32),
        compiler_params=_cparams("parallel", "parallel"),
        name="ada_mod",
    )(cc, ada_w, ada_b.reshape(depth, 1, n))


class _Dims:
    def __init__(self, B, L, C, D):
        self.B, self.L, self.C, self.D = B, L, C, D
        self.n_lat = B * L
        self.T = B * L + B * C

    def mod_row(self, tm):
        return lambda i: jnp.minimum((i * tm) // self.L, self.B)


def _mod_spec(dims, tm, chunk, d):
    row = dims.mod_row(tm)
    return pl.BlockSpec((1, 1, d), lambda i: (row(i) * 6 + chunk, 0, 0))


def _row_spec(tm, n):
    return pl.BlockSpec((tm, n), lambda i: (i, 0))


def _const_spec(shape):
    return pl.BlockSpec(shape, lambda i: (0,) * len(shape))


def _gqa_qkv_body(x_ref, g_ref, sh_ref, sc_ref, w_ref, b_ref, wvt_ref, bv_ref, cos_ref, sin_ref,
                  q_ref, k_ref, vt_ref):
    h = _norm_mod(x_ref[...], g_ref[...], sh_ref[0], sc_ref[0]).astype(BF16)
    cos, sin = cos_ref[...], sin_ref[...]
    tm = h.shape[0]
    nq = GQA_HEADS * GQA_HEAD_DIM
    nkv = GQA_KV_HEADS * GQA_HEAD_DIM
    dh = GQA_HEAD_DIM
    scale = dh ** -0.5
    for j in range(nq // 256):
        z = _dot(h, w_ref[:, j * 256:(j + 1) * 256]) + b_ref[:, j * 256:(j + 1) * 256]
        for c in range(2):
            r = _rope128(z[:, c * LANES:(c + 1) * LANES], cos, sin) * scale
            q_ref[:, j * 256 + c * LANES:j * 256 + (c + 1) * LANES] = r.astype(BF16)
    zk = _dot(h, w_ref[:, nq:nq + nkv]) + b_ref[:, nq:nq + nkv]
    for c in range(nkv // LANES):
        r = _rope128(zk[:, c * LANES:(c + 1) * LANES], cos, sin)
        k_ref[2 * c] = r[:, :dh].astype(BF16)
        k_ref[2 * c + 1] = r[:, dh:].astype(BF16)
    zvt = _dot_nt(wvt_ref[...], h) + bv_ref[...]
    ones_rows = jnp.where(lax.broadcasted_iota(I32, (GQA_VT_ROWS - dh, tm), 0) == 0, 1.0, 0.0).astype(BF16)
    for kv in range(GQA_KV_HEADS):
        vt_ref[kv, :dh, :] = zvt[kv * dh:(kv + 1) * dh, :].astype(BF16)
        vt_ref[kv, dh:, :] = ones_rows


def _gqa_qkv(dims, x, g, mod, w_qk, b_qk, wvt, bv, cos, sin, tm):
    T, D = dims.T, dims.D
    n = w_qk.shape[1]
    nq = GQA_HEADS * GQA_HEAD_DIM
    nkv = GQA_KV_HEADS * GQA_HEAD_DIM
    return pl.pallas_call(
        _gqa_qkv_body,
        grid=(T // tm,),
        in_specs=[_row_spec(tm, D), _const_spec((1, D)), _mod_spec(dims, tm, 0, D), _mod_spec(dims, tm, 1, D),
                  _const_spec((D, n)), _const_spec((1, n)), _const_spec((nkv, D)), _const_spec((nkv, 1)),
                  _row_spec(tm, LANES), _row_spec(tm, LANES)],
        out_specs=[_row_spec(tm, nq),
                   pl.BlockSpec((GQA_KV_HEADS, tm, GQA_HEAD_DIM), lambda i: (0, i, 0)),
                   pl.BlockSpec((GQA_KV_HEADS, GQA_VT_ROWS, tm), lambda i: (0, 0, i))],
        out_shape=[jax.ShapeDtypeStruct((T, nq), BF16),
                   jax.ShapeDtypeStruct((GQA_KV_HEADS, T, GQA_HEAD_DIM), BF16),
                   jax.ShapeDtypeStruct((GQA_KV_HEADS, GQA_VT_ROWS, T), BF16)],
        compiler_params=_cparams("parallel"),
        name="gqa_qkv",
    )(x, g, mod, mod, w_qk, b_qk, wvt, bv, cos, sin)


def _gqa_attn_body(sink_ref, q_ref, kp_ref, km_ref, kn_ref, kc_ref, vp_ref, vm_ref, vn_ref, vc_ref, o_ref, ot_ref,
                   *, tq, L, local):
    n = pl.program_id(1)
    kv = pl.program_id(2)
    dh = GQA_HEAD_DIM
    kc, vtc = kc_ref[0], vc_ref[0]
    if local:
        kl = jnp.concatenate([kp_ref[0], km_ref[0], kn_ref[0]], axis=0)
        vtl = jnp.concatenate([vp_ref[0], vm_ref[0], vn_ref[0]], axis=1)
        nk = tq + 2 * WINDOW
        rel = lax.broadcasted_iota(I32, (nk, tq), 0) - lax.broadcasted_iota(I32, (nk, tq), 1)
        pos = n * tq - WINDOW + lax.broadcasted_iota(I32, (nk, tq), 0)
        valid = (rel >= 0) & (rel <= 2 * WINDOW) & (pos >= 0) & (pos < L)

    def scores(gq):
        qh = q_ref[:, gq * dh:(gq + 1) * dh]
        return _dot_nt(kc, qh), (_dot_nt(kl, qh) if local else None)

    ahead = [scores(gq) for gq in range(GQA_LOOKAHEAD)]
    for gq in range(GQA_GROUP):
        s_ctx, s_loc = ahead.pop(0)
        if gq + GQA_LOOKAHEAD < GQA_GROUP:
            ahead.append(scores(gq + GQA_LOOKAHEAD))
        sk = sink_ref[kv * GQA_GROUP + gq]
        m = jnp.maximum(jnp.max(s_ctx, axis=0, keepdims=True), sk)
        if local:
            s_loc = jnp.where(valid, s_loc, -jnp.inf)
            m = jnp.maximum(m, jnp.max(s_loc, axis=0, keepdims=True))
        acc = _dot(vtc, jnp.exp((s_ctx - m).astype(BF16)))
        if local:
            acc = acc + _dot(vtl, jnp.exp((s_loc - m).astype(BF16)))
        den = acc[dh:dh + 1] + jnp.exp(sk - m)
        ot_ref[gq * dh:(gq + 1) * dh, :] = acc[:dh] / den
    o_ref[...] = ot_ref[...].T.astype(BF16)


def _gqa_attn(dims, q, k, vt, sink, *, local, tq):
    B, L, C, T = dims.B, dims.L, dims.C, dims.T
    gw = GQA_GROUP * GQA_HEAD_DIM
    hb = WINDOW
    if local:
        nq, row0, n_out = L // tq, lambda b, n: b * (L // tq) + n, B * L
    else:
        nq, row0, n_out = 1, lambda b, n: (B * L) // tq + b, B * C
    out_row0 = (lambda b, n: row0(b, n)) if local else (lambda b, n: b)
    last_hb = T // hb - 1

    def q_map(b, n, kv, s):
        return (row0(b, n), kv)

    def o_map(b, n, kv, s):
        return (out_row0(b, n), kv)

    def main_blk(b, n):
        return row0(b, n)

    def prev_blk(b, n):
        return jnp.maximum(row0(b, n) * (tq // hb) - 1, 0)

    def next_blk(b, n):
        return jnp.minimum((row0(b, n) + 1) * (tq // hb), last_hb)

    def ctx_blk(b, n):
        return (B * L) // C + b

    def k_spec(rows, blk):
        return pl.BlockSpec((1, rows, GQA_HEAD_DIM), lambda b, n, kv, s: (kv, blk(b, n), 0))

    def vt_spec(cols, blk):
        return pl.BlockSpec((1, GQA_VT_ROWS, cols), lambda b, n, kv, s: (kv, 0, blk(b, n)))

    blocks = [(hb, prev_blk), (tq, main_blk), (hb, next_blk), (C, ctx_blk)]
    grid_spec = pltpu.PrefetchScalarGridSpec(
        num_scalar_prefetch=1,
        grid=(B, nq, GQA_KV_HEADS),
        in_specs=[pl.BlockSpec((tq, gw), q_map)] + [k_spec(*x) for x in blocks] + [vt_spec(*x) for x in blocks],
        out_specs=pl.BlockSpec((tq, gw), o_map),
        scratch_shapes=[pltpu.VMEM((gw, tq), F32)],
    )
    return pl.pallas_call(
        functools.partial(_gqa_attn_body, tq=tq, L=L, local=local),
        grid_spec=grid_spec, out_shape=jax.ShapeDtypeStruct((n_out, q.shape[1]), BF16),
        compiler_params=_cparams("parallel", "parallel", "parallel"),
        name="gqa_attn_local" if local else "gqa_attn_ctx",
    )(sink, q, k, k, k, k, vt, vt, vt, vt)


def _proj_res_body(*refs, tn, n_lat_tiles, with_ctx):
    if with_ctx:
        al_ref, ac_ref, w_ref, b_ref, x_ref, gate_ref, o_ref = refs
    else:
        al_ref, w_ref, b_ref, x_ref, gate_ref, o_ref = refs
    gate = gate_ref[0]

    def run(a_ref):
        a = a_ref[...]
        for j in range(o_ref.shape[1] // tn):
            sl = slice(j * tn, (j + 1) * tn)
            y = _dot(a, w_ref[:, sl]) + b_ref[:, sl]
            o_ref[:, sl] = x_ref[:, sl] + gate[:, sl] * y

    if with_ctx:
        is_lat = pl.program_id(0) < n_lat_tiles
        pl.when(is_lat)(lambda: run(al_ref))
        pl.when(jnp.logical_not(is_lat))(lambda: run(ac_ref))
    else:
        run(al_ref)


def _proj_res(dims, a_lat, a_ctx, w, b, x, mod, chunk, tm):
    T, D = dims.T, dims.D
    K = a_lat.shape[1]
    n_lat_tiles = dims.n_lat // tm
    with_ctx = a_ctx is not None
    a_specs = [pl.BlockSpec((tm, K), lambda i: (jnp.minimum(i, n_lat_tiles - 1), 0))]
    args = [a_lat]
    if with_ctx:
        a_specs.append(pl.BlockSpec((tm, K), lambda i: (jnp.maximum(i - n_lat_tiles, 0), 0)))
        args.append(a_ctx)
    return pl.pallas_call(
        functools.partial(_proj_res_body, tn=512, n_lat_tiles=n_lat_tiles, with_ctx=with_ctx),
        grid=(T // tm if with_ctx else n_lat_tiles,),
        in_specs=a_specs + [_const_spec((K, D)), _const_spec((1, D)), _row_spec(tm, D),
                            _mod_spec(dims, tm, chunk, D)],
        out_specs=_row_spec(tm, D),
        out_shape=jax.ShapeDtypeStruct((T, D), F32),
        input_output_aliases={len(args) + 2: 0},
        compiler_params=_cparams("parallel"),
        name="proj_residual",
    )(*args, w, b, x, mod)


def _pool_body(xp_ref, x_ref, xn_ref, g_ref, sh_ref, sc_ref, gate_ref, w_ref, b_ref, ps_ref, o_ref, hf_ref,
               *, tm, dims):
    i = pl.program_id(0)
    row0 = i * tm
    is_lat = row0 < dims.n_lat
    n = jnp.where(is_lat, dims.L, dims.C)
    pos0 = jnp.where(is_lat, row0 % dims.L, (row0 - dims.n_lat) % dims.C)
    g, sh, sc = g_ref[...], sh_ref[0], sc_ref[0]
    x = x_ref[...]
    h = _norm_mod(x, g, sh, sc)
    hp = _norm_mod(xp_ref[...], g, sh, sc)
    hn = _norm_mod(xn_ref[...], g, sh, sc)
    hf_ref[0:POOL_HALO, :] = jnp.where(pos0 > 0, hp, 0.0)
    hf_ref[POOL_HALO:POOL_HALO + tm, :] = h
    hf_ref[POOL_HALO + tm:, :] = jnp.where(pos0 + tm < n, hn, 0.0)
    pos = pos0 + lax.broadcasted_iota(I32, (tm, 1), 0)
    gate = gate_ref[0]
    gw = dims.D // len(POOL_WINDOWS)
    for gi, win in enumerate(POOL_WINDOWS):
        half = win // 2
        sl = slice(gi * gw, (gi + 1) * gw)
        acc = hf_ref[POOL_HALO - half:POOL_HALO - half + tm, sl]
        for o in range(-half + 1, half):
            acc = acc + hf_ref[POOL_HALO + o:POOL_HALO + o + tm, sl]
        cnt = (jnp.minimum(pos + half, n) - jnp.maximum(pos - half, 0)).astype(F32)
        d = (acc / cnt - h[:, sl]).astype(BF16)
        y = (_dot(d, w_ref[gi]) + b_ref[:, sl]) * ps_ref[:, sl]
        o_ref[:, sl] = x[:, sl] + gate[:, sl] * y


def _pool(dims, x, g, mod, w, b, ps, tm):
    T, D = dims.T, dims.D
    hb = tm // POOL_HALO
    last = T // POOL_HALO - 1
    gw = D // len(POOL_WINDOWS)
    return pl.pallas_call(
        functools.partial(_pool_body, tm=tm, dims=dims),
        grid=(T // tm,),
        in_specs=[pl.BlockSpec((POOL_HALO, D), lambda i: (jnp.maximum(i * hb - 1, 0), 0)),
                  _row_spec(tm, D),
                  pl.BlockSpec((POOL_HALO, D), lambda i: (jnp.minimum((i + 1) * hb, last), 0)),
                  _const_spec((1, D)), _mod_spec(dims, tm, 0, D), _mod_spec(dims, tm, 1, D),
                  _mod_spec(dims, tm, 2, D), _const_spec((len(POOL_WINDOWS), gw, gw)),
                  _const_spec((1, D)), _const_spec((1, D))],
        out_specs=_row_spec(tm, D),
        out_shape=jax.ShapeDtypeStruct((T, D), F32),
        scratch_shapes=[pltpu.VMEM((tm + 2 * POOL_HALO, D), F32)],
        compiler_params=_cparams("parallel"),
        name="pool_mixer",
    )(x, x, x, g, mod, mod, mod, w, b, ps)


def _mla_down_body(x_ref, g_ref, sh_ref, sc_ref, w_ref, gq_ref, gkv_ref, cos_ref, sin_ref,
                   cq_ref, ckv_ref, kr_ref):
    h = _norm_mod(x_ref[...], g_ref[...], sh_ref[0], sc_ref[0]).astype(BF16)

    def rms(z, gain):
        return (z * lax.rsqrt(jnp.mean(z * z, axis=-1, keepdims=True) + EPS) * gain).astype(BF16)

    cq_ref[...] = rms(_dot(h, w_ref[:, :MLA_Q_RANK]), gq_ref[...])
    ckv_ref[...] = rms(_dot(h, w_ref[:, MLA_Q_RANK:MLA_Q_RANK + MLA_KV_RANK]), gkv_ref[...])
    zr = _dot(h, w_ref[:, MLA_Q_RANK + MLA_KV_RANK:])
    kr_ref[...] = _rope128(zr, cos_ref[...], sin_ref[...])[:, :MLA_ROPE].astype(BF16)


def _mla_down(dims, x, g, mod, w_pad, gq, gkv, cos, sin, tm):
    T, D = dims.T, dims.D
    n = w_pad.shape[1]
    return pl.pallas_call(
        _mla_down_body,
        grid=(T // tm,),
        in_specs=[_row_spec(tm, D), _const_spec((1, D)), _mod_spec(dims, tm, 0, D), _mod_spec(dims, tm, 1, D),
                  _const_spec((D, n)), _const_spec((1, MLA_Q_RANK)), _const_spec((1, MLA_KV_RANK)),
                  _row_spec(tm, LANES), _row_spec(tm, LANES)],
        out_specs=[_row_spec(tm, MLA_Q_RANK), _row_spec(tm, MLA_KV_RANK), _row_spec(tm, MLA_ROPE)],
        out_shape=[jax.ShapeDtypeStruct((T, MLA_Q_RANK), BF16), jax.ShapeDtypeStruct((T, MLA_KV_RANK), BF16),
                   jax.ShapeDtypeStruct((T, MLA_ROPE), BF16)],
        compiler_params=_cparams("parallel"),
        name="mla_down",
    )(x, g, mod, mod, w_pad, gq, gkv, cos, sin)


def _mla_up_body(cq_ref, ckv_ref, kr_ref, cos_ref, sin_ref, wqn_ref, wqr_ref, wk_ref, wvt_ref,
                 q_ref, k_ref, vt_ref):
    cq, ckv, kr = cq_ref[...], ckv_ref[...], kr_ref[...]
    cos, sin = cos_ref[...], sin_ref[...]
    scale = MLA_QK ** -0.5
    H = MLA_HEADS
    tm = cq.shape[0]
    ones_rows = jnp.where(lax.broadcasted_iota(I32, (MLA_VT_ROWS - MLA_V, tm), 0) == 0, 1.0, 0.0).astype(BF16)
    for j in range(H // 2):
        zq = _dot(cq, wqn_ref[:, j * 256:(j + 1) * 256]) * scale
        zk = _dot(ckv, wk_ref[:, j * 256:(j + 1) * 256])
        for c in range(2):
            hh = 2 * j + c
            q_ref[hh, :, :MLA_NOPE] = zq[:, c * LANES:(c + 1) * LANES].astype(BF16)
            k_ref[hh, :, :MLA_NOPE] = zk[:, c * LANES:(c + 1) * LANES].astype(BF16)
            k_ref[hh, :, MLA_NOPE:] = kr
            vt_ref[hh, :MLA_V, :] = _dot_nt(wvt_ref[hh * MLA_V:(hh + 1) * MLA_V, :], ckv).astype(BF16)
            vt_ref[hh, MLA_V:, :] = ones_rows
    for j in range(H // 4):
        zr = _dot(cq, wqr_ref[:, j * 256:(j + 1) * 256])
        for c in range(2):
            r = (_rope128(zr[:, c * LANES:(c + 1) * LANES], cos, sin) * scale).astype(BF16)
            q_ref[4 * j + 2 * c, :, MLA_NOPE:] = r[:, :MLA_ROPE]
            q_ref[4 * j + 2 * c + 1, :, MLA_NOPE:] = r[:, MLA_ROPE:]


def _mla_up(dims, cq, ckv, kr, cos, sin, wqn, wqr, wk, wvt, tm):
    T = dims.T
    H = MLA_HEADS

    def hspec(n):
        return pl.BlockSpec((H, tm, n), lambda i: (0, i, 0))

    return pl.pallas_call(
        _mla_up_body,
        grid=(T // tm,),
        in_specs=[_row_spec(tm, MLA_Q_RANK), _row_spec(tm, MLA_KV_RANK), _row_spec(tm, MLA_ROPE),
                  _row_spec(tm, LANES), _row_spec(tm, LANES),
                  _const_spec(wqn.shape), _const_spec(wqr.shape), _const_spec(wk.shape), _const_spec(wvt.shape)],
        out_specs=[hspec(MLA_QK), hspec(MLA_QK), pl.BlockSpec((H, MLA_VT_ROWS, tm), lambda i: (0, 0, i))],
        out_shape=[jax.ShapeDtypeStruct((H, T, MLA_QK), BF16), jax.ShapeDtypeStruct((H, T, MLA_QK), BF16),
                   jax.ShapeDtypeStruct((H, MLA_VT_ROWS, T), BF16)],
        compiler_params=_cparams("parallel"),
        name="mla_up",
    )(cq, ckv, kr, cos, sin, wqn, wqr, wk, wvt)


def _mla_attn_body(*refs, tk, n_chunks, latent_keys):
    if latent_keys:
        q_ref, kc_ref, vc_ref, kl_ref, vl_ref, o_ref, acc_ref = refs
    else:
        q_ref, kc_ref, vc_ref, o_ref, acc_ref = refs
    q = q_ref[0]

    def scores(c):
        return _dot_nt(kl_ref[0, c * tk:(c + 1) * tk, :], q)

    s = _dot_nt(kc_ref[0], q)
    ahead = [scores(c) for c in range(min(MLA_LOOKAHEAD, n_chunks))] if latent_keys else []
    m = jnp.max(s, axis=0, keepdims=True)
    p = jnp.exp((s - m).astype(BF16))
    acc_ref[...] = _dot(vc_ref[0], p)
    if latent_keys:
        for c in range(n_chunks):
            s = ahead.pop(0)
            if c + MLA_LOOKAHEAD < n_chunks:
                ahead.append(scores(c + MLA_LOOKAHEAD))
            m2 = jnp.maximum(m, jnp.max(s, axis=0, keepdims=True))
            p = jnp.exp((s - m2).astype(BF16))
            acc_ref[...] = jnp.exp(m - m2) * acc_ref[...] + _dot(vl_ref[0, :, c * tk:(c + 1) * tk], p)
            m = m2
    acc = acc_ref[...]
    o_ref[...] = (acc[:MLA_V] / acc[MLA_V:MLA_V + 1]).T.astype(BF16)


def _mla_attn(dims, q, k, vt, *, latent_queries, tq, tk):
    B, L, C = dims.B, dims.L, dims.C
    H = MLA_HEADS
    if latent_queries:
        nq, row0, n_out = L // tq, lambda b, n: b * (L // tq) + n, B * L
    else:
        nq, row0, n_out = 1, lambda b, n: (B * L) // tq + b, B * C
    out_row0 = row0 if latent_queries else (lambda b, n: b)

    def q_map(b, h, n):
        return (h, row0(b, n), 0)

    def o_map(b, h, n):
        return (out_row0(b, n), h)

    in_specs = [pl.BlockSpec((1, tq, MLA_QK), q_map),
                pl.BlockSpec((1, C, MLA_QK), lambda b, h, n: (h, (B * L) // C + b, 0)),
                pl.BlockSpec((1, MLA_VT_ROWS, C), lambda b, h, n: (h, 0, (B * L) // C + b))]
    args = [q, k, vt]
    if latent_queries:
        in_specs += [pl.BlockSpec((1, L, MLA_QK), lambda b, h, n: (h, b, 0)),
                     pl.BlockSpec((1, MLA_VT_ROWS, L), lambda b, h, n: (h, 0, b))]
        args += [k, vt]
    return pl.pallas_call(
        functools.partial(_mla_attn_body, tk=tk, n_chunks=L // tk, latent_keys=latent_queries),
        grid=(B, H, nq),
        in_specs=in_specs,
        out_specs=pl.BlockSpec((tq, MLA_V), o_map),
        out_shape=jax.ShapeDtypeStruct((n_out, H * MLA_V), BF16),
        scratch_shapes=[pltpu.VMEM((MLA_VT_ROWS, tq), F32)],
        compiler_params=_cparams("parallel", "parallel", "arbitrary"),
        name="mla_attn_latent" if latent_queries else "mla_attn_ctx",
    )(*args)


def _router_body(x_ref, g_ref, sh_ref, sc_ref, whi_ref, wlo_ref, br_ref, hn_ref, rho_ref, gate_ref, cnt_ref):
    h = _norm_mod(x_ref[...], g_ref[...], sh_ref[0], sc_ref[0])
    hi = h.astype(BF16)
    lo = (h - hi.astype(F32)).astype(BF16)
    hn_ref[...] = hi
    whi, wlo = whi_ref[...], wlo_ref[...]
    lt = _dot_nt(whi, hi) + _dot_nt(wlo, hi) + _dot_nt(whi, lo) + br_ref[...]
    E = N_EXPERTS
    eidx = lax.broadcasted_iota(I32, (E, SUB), 0)
    onehots, tops = [], []
    for _ in range(TOP_K):
        m = jnp.max(lt, axis=0, keepdims=True)
        idx = jnp.min(jnp.where(lt == m, eidx, E), axis=0, keepdims=True)
        oh = eidx == idx
        onehots.append(oh)
        tops.append(m)
        lt = jnp.where(oh, -jnp.inf, lt)
    ex = [jnp.exp(t - tops[0]) for t in tops]
    den = ex[0] + ex[1] + ex[2] + ex[3]
    chosen = (onehots[0] | onehots[1] | onehots[2] | onehots[3])
    chosen_f = jnp.where(chosen, 1.0, 0.0)
    cnt = jnp.sum(chosen_f, axis=1, keepdims=True)
    units = jnp.floor((cnt + (UNIT - 1)) / UNIT)
    cnt_ref[0] = units
    lower = (lax.broadcasted_iota(I32, (E, E), 1) < lax.broadcasted_iota(I32, (E, E), 0))
    start = _dot(jnp.where(lower, 1.0, 0.0).astype(BF16),
                 jnp.broadcast_to(units, (E, LANES)).astype(BF16))[:, :1] * UNIT
    before = (lax.broadcasted_iota(I32, (SUB, SUB), 0) < lax.broadcasted_iota(I32, (SUB, SUB), 1))
    rank = _dot(chosen_f.astype(BF16), jnp.where(before, 1.0, 0.0).astype(BF16))
    row = start + rank
    for k in range(TOP_K):
        rho_ref[k:k + 1, :] = jnp.sum(jnp.where(onehots[k], row, 0.0), axis=0, keepdims=True).astype(I32)
        gate_ref[k:k + 1, :] = ex[k] / den


def _router(dims, x, g, mod, whi, wlo, br):
    T, D = dims.T, dims.D
    n_sub = T // SUB
    E = N_EXPERTS
    return pl.pallas_call(
        _router_body,
        grid=(n_sub,),
        in_specs=[_row_spec(SUB, D), _const_spec((1, D)), _mod_spec(dims, SUB, 3, D), _mod_spec(dims, SUB, 4, D),
                  _const_spec((E, D)), _const_spec((E, D)), _const_spec((E, 1))],
        out_specs=[_row_spec(SUB, D), pl.BlockSpec((TOP_K, SUB), lambda i: (0, i)),
                   pl.BlockSpec((TOP_K, SUB), lambda i: (0, i)), pl.BlockSpec((1, E, 1), lambda i: (i, 0, 0))],
        out_shape=[jax.ShapeDtypeStruct((T, D), BF16), jax.ShapeDtypeStruct((TOP_K, T), I32),
                   jax.ShapeDtypeStruct((TOP_K, T), F32), jax.ShapeDtypeStruct((n_sub, E, 1), F32)],
        compiler_params=_cparams("parallel"),
        name="moe_router",
    )(x, g, mod, mod, whi, wlo, br)


def _dispatch_body(hn_ref, rho_ref, a_ref, *, n_sub):
    u = pl.program_id(0)
    rows = lax.broadcasted_iota(I32, (SUB_ROWS, SUB), 0)
    hit = rows == rho_ref[0:1, :]
    for k in range(1, TOP_K):
        hit = hit | (rows == rho_ref[k:k + 1, :])
    sel = jnp.where(hit & (u < n_sub), 1.0, 0.0).astype(BF16)
    a_ref[...] = _dot(sel, hn_ref[...]).astype(BF16)


def _dispatch(dims, hn, rho):
    T, D = dims.T, dims.D
    n_sub = T // SUB
    last = n_sub - 1
    return pl.pallas_call(
        functools.partial(_dispatch_body, n_sub=n_sub),
        grid=(n_sub + 1,),
        in_specs=[pl.BlockSpec((SUB, D), lambda u: (jnp.minimum(u, last), 0)),
                  pl.BlockSpec((TOP_K, SUB), lambda u: (0, jnp.minimum(u, last)))],
        out_specs=pl.BlockSpec((SUB_ROWS, D), lambda u: (u, 0)),
        out_shape=jax.ShapeDtypeStruct(((n_sub + 1) * SUB_ROWS, D), BF16),
        compiler_params=_cparams("parallel"),
        name="moe_dispatch",
    )(hn, rho)


def _expert_body(src_ref, be_ref, nu_ref, *refs):
    xs = refs[:BLOCK_UNITS]
    w1_ref, b1_ref, w2_ref, b2_ref, y_ref = refs[BLOCK_UNITS:]
    j = pl.program_id(0)

    @pl.when(j < nu_ref[0])
    def _():
        x = jnp.concatenate([r[...] for r in xs], axis=0)
        a = _dot(x, w1_ref[0]) + b1_ref[0]
        a_glu = jnp.minimum(a[:, :EXPERT_FF], SWIGLU_LIMIT)
        a_lin = jnp.clip(a[:, EXPERT_FF:], -SWIGLU_LIMIT, SWIGLU_LIMIT)
        u = a_glu * jax.nn.sigmoid(SWIGLU_ALPHA * a_glu) * (a_lin + 1.0)
        y_ref[...] = (_dot(u.astype(BF16), w2_ref[0]) + b2_ref[0]).astype(BF16)

    @pl.when(j >= nu_ref[0])
    def _():
        y_ref[...] = jnp.zeros_like(y_ref)


def _experts(a, src_unit, blk_e, n_used, layer, w1, b1, w2, b2, n_blocks):
    D = a.shape[1]
    ff2 = w1.shape[2]

    def unit_map(i, j, src, be, nu):
        return (src[j * BLOCK_UNITS + i], 0)

    def e_map(j, src, be, nu):
        return (layer * N_EXPERTS + be[j], 0, 0)

    grid_spec = pltpu.PrefetchScalarGridSpec(
        num_scalar_prefetch=3,
        grid=(n_blocks,),
        in_specs=[pl.BlockSpec((UNIT, D), functools.partial(unit_map, i)) for i in range(BLOCK_UNITS)]
        + [pl.BlockSpec((1, D, ff2), e_map), pl.BlockSpec((1, 1, ff2), e_map),
           pl.BlockSpec((1, EXPERT_FF, D), e_map), pl.BlockSpec((1, 1, D), e_map)],
        out_specs=pl.BlockSpec((EXPERT_BLOCK, D), lambda j, src, be, nu: (j, 0)),
    )
    return pl.pallas_call(
        _expert_body,
        grid_spec=grid_spec,
        out_shape=jax.ShapeDtypeStruct((n_blocks * EXPERT_BLOCK, D), BF16),
        compiler_params=_cparams("arbitrary"),
        name="moe_experts",
    )(src_unit, blk_e, n_used, *([a] * BLOCK_UNITS), w1, b1, w2, b2)


def _combine_body(dst_ref, x_ref, rho_ref, gate_ref, g2_ref, ys_ref, o_ref, buf_ref, sem_ref, *, n_sub):
    u = pl.program_id(0)

    def unit_copy(uu, slot, i):
        d = dst_ref[uu * SUB_UNITS + i]
        return d, pltpu.make_async_copy(ys_ref.at[pl.ds(pl.multiple_of(jnp.maximum(d, 0) * UNIT, UNIT), UNIT)],
                                        buf_ref.at[slot, pl.ds(pl.multiple_of(i * UNIT, UNIT), UNIT)],
                                        sem_ref.at[slot])

    def start_all(uu, slot):
        def f(i, c):
            d, cp = unit_copy(uu, slot, i)

            @pl.when(d >= 0)
            def _():
                cp.start()
            return c
        lax.fori_loop(0, SUB_UNITS, f, 0)

    def wait_all(uu, slot):
        def f(i, c):
            d, cp = unit_copy(uu, slot, i)

            @pl.when(d >= 0)
            def _():
                cp.wait()
            return c
        lax.fori_loop(0, SUB_UNITS, f, 0)

    @pl.when(u == 0)
    def _():
        buf_ref[...] = jnp.zeros_like(buf_ref)
        start_all(0, 0)

    @pl.when(u + 1 < n_sub)
    def _():
        start_all(u + 1, (u + 1) % 2)

    slot = u % 2
    wait_all(u, slot)
    cols = lax.broadcasted_iota(I32, (SUB, SUB_ROWS), 1)
    wgt = jnp.where(cols == rho_ref[:, 0:1], gate_ref[:, 0:1], 0.0)
    for k in range(1, TOP_K):
        wgt = wgt + jnp.where(cols == rho_ref[:, k:k + 1], gate_ref[:, k:k + 1], 0.0)
    y = _dot(wgt.astype(BF16), buf_ref[slot])
    o_ref[...] = x_ref[...] + g2_ref[0] * y


def _combine(dims, dst_unit, x, rho_t, gate_t, mod, ys):
    T, D = dims.T, dims.D
    n_sub = T // SUB
    grid_spec = pltpu.PrefetchScalarGridSpec(
        num_scalar_prefetch=1,
        grid=(n_sub,),
        in_specs=[pl.BlockSpec((SUB, D), lambda u, dst: (u, 0)),
                  pl.BlockSpec((SUB, TOP_K), lambda u, dst: (u, 0)),
                  pl.BlockSpec((SUB, TOP_K), lambda u, dst: (u, 0)),
                  pl.BlockSpec((1, 1, D), lambda u, dst: (dims.mod_row(SUB)(u) * 6 + 5, 0, 0)),
                  pl.BlockSpec(memory_space=pl.ANY)],
        out_specs=pl.BlockSpec((SUB, D), lambda u, dst: (u, 0)),
        scratch_shapes=[pltpu.VMEM((2, SUB_ROWS, D), BF16), pltpu.SemaphoreType.DMA((2,))],
    )
    return pl.pallas_call(
        functools.partial(_combine_body, n_sub=n_sub),
        grid_spec=grid_spec,
        out_shape=jax.ShapeDtypeStruct((T, D), F32),
        input_output_aliases={1: 0},
        compiler_params=_cparams("arbitrary"),
        name="moe_combine",
    )(dst_unit, x, rho_t, gate_t, mod, ys)


def _moe_tables(units, n_blocks):
    n_sub, E = units.shape
    g = units.astype(I32)
    off_a = jnp.cumsum(g, axis=1) - g
    tot = jnp.sum(g, axis=0)
    pad = (tot + BLOCK_UNITS - 1) // BLOCK_UNITS * BLOCK_UNITS
    pend = jnp.cumsum(pad)
    pos_x = (pend - pad)[None, :] + jnp.cumsum(g, axis=0) - g
    sub_base = jnp.arange(n_sub, dtype=I32)[:, None] * SUB_UNITS + off_a
    p = jnp.arange(n_blocks * BLOCK_UNITS, dtype=I32)[:, None]
    lo, cnt, delta = pos_x.reshape(1, -1), g.reshape(1, -1), (sub_base - pos_x).reshape(1, -1)
    inside = (p >= lo) & (p < lo + cnt)
    src_unit = jnp.where(jnp.any(inside, axis=1), jnp.sum(jnp.where(inside, delta + p, 0), axis=1),
                         n_sub * SUB_UNITS).astype(I32)
    blk_e = jnp.minimum(jnp.sum(jnp.arange(n_blocks, dtype=I32)[:, None] * BLOCK_UNITS >= pend[None, :], axis=1),
                        E - 1).astype(I32)
    n_used = (pend[-1:] // BLOCK_UNITS).astype(I32)
    i = jnp.arange(SUB_UNITS, dtype=I32)[None, :, None]
    inside = (i >= off_a[:, None, :]) & (i < (off_a + g)[:, None, :])
    dst = jnp.sum(jnp.where(inside, (pos_x - off_a)[:, None, :] + i, 0), axis=-1)
    dst_unit = jnp.where(jnp.any(inside, axis=-1), dst, -1).astype(I32).reshape(-1)
    return src_unit, blk_e, n_used, dst_unit


def _moe(dims, x, g, mod, whi, wlo, br, layer, w1, b1, w2, b2):
    T = dims.T
    n_sub = T // SUB
    n_blocks = (n_sub * SUB_UNITS + N_EXPERTS * (BLOCK_UNITS - 1)) // BLOCK_UNITS + 1
    hn, rho, gate, units = _router(dims, x, g, mod, whi, wlo, br)
    src_unit, blk_e, n_used, dst_unit = _moe_tables(units.reshape(n_sub, N_EXPERTS), n_blocks)
    a = _dispatch(dims, hn, rho)
    ys = _experts(a, src_unit, blk_e, n_used, layer, w1, b1, w2, b2, n_blocks)
    return _combine(dims, dst_unit, x, rho.T, gate.T, mod, ys)


def _final_body(x_ref, g_ref, o_ref):
    x = x_ref[...]
    o_ref[...] = x * lax.rsqrt(jnp.mean(x * x, axis=-1, keepdims=True) + EPS) * g_ref[...]


def _final_norm(x, g, n_rows, tm):
    D = x.shape[1]
    return pl.pallas_call(
        _final_body,
        grid=(n_rows // tm,),
        in_specs=[_row_spec(tm, D), _const_spec((1, D))],
        out_specs=_row_spec(tm, D),
        out_shape=jax.ShapeDtypeStruct((n_rows, D), F32),
        compiler_params=_cparams("parallel"),
        name="final_norm",
    )(x, g)


def _rope_tables(B, L, C):
    half = GQA_HEAD_DIM // 4
    inv = ROPE_BASE ** (-jnp.arange(half, dtype=F32) / half)
    t = jnp.arange(L)
    ang_r = (t // GRID_W).astype(F32)[:, None] * inv[None, :]
    ang_c = (t % GRID_W).astype(F32)[:, None] * inv[None, :]
    cos = jnp.concatenate([jnp.cos(ang_r)] * 2 + [jnp.cos(ang_c)] * 2, axis=-1)
    sin = jnp.concatenate([-jnp.sin(ang_r), jnp.sin(ang_r), -jnp.sin(ang_c), jnp.sin(ang_c)], axis=-1)
    cos = jnp.tile(cos, (B, LANES // cos.shape[1]))
    sin = jnp.tile(sin, (B, LANES // sin.shape[1]))
    cos = jnp.concatenate([cos, jnp.ones((B * C, LANES), F32)], axis=0)
    sin = jnp.concatenate([sin, jnp.zeros((B * C, LANES), F32)], axis=0)
    return cos, sin


def kernel(x, c, ctx, c_ctx, ada_w, ada_b, norm_g, final_g, gqa_w_qkv, gqa_b_qkv, gqa_sink, gqa_w_o, gqa_b_o,
           pool_w, pool_b, pool_scale, mla_w_down, mla_g_q, mla_w_uq, mla_g_kv, mla_w_ukv, mla_w_o,
           router_w, router_b, exp_w1, exp_b1, exp_w2, exp_b2):
    B, L, D = x.shape
    C = ctx.shape[1]
    depth = ada_w.shape[0]
    dims = _Dims(B, L, C, D)
    T = dims.T
    tm = 512 if (L % 512 == 0 and (B * C) % 512 == 0) else 256
    assert L % tm == 0 and (B * C) % tm == 0 and L % SUB == 0 and C % SUB == 0 and B + 1 <= UNIT
    assert L % GRID_W == 0 and SUB % WINDOW == 0 and C % WINDOW == 0

    cc = jnp.zeros((UNIT, D), F32).at[:B].set(c).at[B].set(c_ctx)
    mods = _ada_mod(cc, ada_w, ada_b)
    cos, sin = _rope_tables(B, L, C)
    xs = jnp.concatenate([x.reshape(B * L, D), ctx.reshape(B * C, D)], axis=0)
    ne = depth * N_EXPERTS
    w1 = exp_w1.astype(BF16).reshape(ne, D, -1)
    w2 = exp_w2.astype(BF16).reshape(ne, -1, D)
    b1 = exp_b1.reshape(ne, 1, -1)
    b2 = exp_b2.reshape(ne, 1, D)

    for i in range(depth):
        need_ctx = i < depth - 1
        mod = mods[i].reshape(UNIT * 6, 1, D)
        kind, j = i % N_MIXERS, i // N_MIXERS
        g1 = norm_g[i, 0].reshape(1, D)
        if kind == 0:
            nqk = (GQA_HEADS + GQA_KV_HEADS) * GQA_HEAD_DIM
            q, k, vt = _gqa_qkv(dims, xs, g1, mod, gqa_w_qkv[j][:, :nqk].astype(BF16),
                                gqa_b_qkv[j][:nqk].reshape(1, -1), gqa_w_qkv[j][:, nqk:].T.astype(BF16),
                                gqa_b_qkv[j][nqk:].reshape(-1, 1), cos, sin, tm)
            o = _gqa_attn(dims, q, k, vt, gqa_sink[j], local=True, tq=SUB)
            oc = _gqa_attn(dims, q, k, vt, gqa_sink[j], local=False, tq=C) if need_ctx else None
            xs = _proj_res(dims, o, oc, gqa_w_o[j].astype(BF16), gqa_b_o[j].reshape(1, D), xs, mod, 2, tm)
        elif kind == 1:
            xs = _pool(dims, xs, g1, mod, pool_w[j].astype(BF16), pool_b[j].reshape(1, D),
                       pool_scale[j].reshape(1, D), SUB)
        else:
            H = MLA_HEADS
            w_down = jnp.pad(mla_w_down[j], ((0, 0), (0, LANES - MLA_ROPE))).astype(BF16)
            wq = mla_w_uq[j].reshape(MLA_Q_RANK, H, MLA_QK)
            wqn = wq[:, :, :MLA_NOPE].reshape(MLA_Q_RANK, H * MLA_NOPE).astype(BF16)
            wqr = wq[:, :, MLA_NOPE:].reshape(MLA_Q_RANK, H * MLA_ROPE).astype(BF16)
            wkv = mla_w_ukv[j].reshape(MLA_KV_RANK, H, MLA_NOPE + MLA_V)
            wk = wkv[:, :, :MLA_NOPE].reshape(MLA_KV_RANK, H * MLA_NOPE).astype(BF16)
            wvt = wkv[:, :, MLA_NOPE:].reshape(MLA_KV_RANK, H * MLA_V).T.astype(BF16)
            cq, ckv, kr = _mla_down(dims, xs, g1, mod, w_down, mla_g_q[j].reshape(1, -1),
                                    mla_g_kv[j].reshape(1, -1), cos, sin, tm)
            q, k, vt = _mla_up(dims, cq, ckv, kr, cos, sin, wqn, wqr, wk, wvt, SUB)
            o = _mla_attn(dims, q, k, vt, latent_queries=True, tq=tm, tk=min(256, L))
            oc = _mla_attn(dims, q, k, vt, latent_queries=False, tq=C, tk=min(256, L)) if need_ctx else None
            xs = _proj_res(dims, o, oc, mla_w_o[j].astype(BF16), jnp.zeros((1, D), F32), xs, mod, 2, tm)
        wr_t = router_w[i].T
        whi = wr_t.astype(BF16)
        wlo = (wr_t - whi.astype(F32)).astype(BF16)
        xs = _moe(dims, xs, norm_g[i, 1].reshape(1, D), mod, whi, wlo, router_b[i].reshape(-1, 1), i, w1, b1, w2, b2)
    out = _final_norm(xs, final_g.reshape(1, D), B * L, tm)
    return out.reshape(B, L, D)
```

```python
import functools

import jax
import jax.numpy as jnp
from jax import lax
from jax.experimental import pallas as pl
from jax.experimental.pallas import tpu as pltpu

F32 = jnp.float32
BF16 = jnp.bfloat16
I32 = jnp.int32

GRID_W = 64
N_MIXERS = 3
EPS = 1e-6
ROPE_BASE = 10000.0

GQA_HEADS = 32
GQA_KV_HEADS = 4
GQA_HEAD_DIM = 64
GQA_GROUP = GQA_HEADS // GQA_KV_HEADS
WINDOW = 128
GQA_LOOKAHEAD = 2
GQA_VT_ROWS = GQA_HEAD_DIM + 16

POOL_WINDOWS = (2, 4, 8, 16)
POOL_HALO = 8

MLA_HEADS = 16
MLA_Q_RANK = 512
MLA_KV_RANK = 512
MLA_NOPE = 128
MLA_ROPE = 64
MLA_V = 128
MLA_QK = MLA_NOPE + MLA_ROPE
MLA_VT_ROWS = MLA_V + 16
MLA_LOOKAHEAD = 2

N_EXPERTS = 32
TOP_K = 4
EXPERT_FF = 768
SWIGLU_LIMIT = 7.0
SWIGLU_ALPHA = 1.702

LANES = 128
MXU_COLS = 256
UNIT = 16
SUB = 256
SUB_UNITS = (SUB * TOP_K + N_EXPERTS * (UNIT - 1) + UNIT - 1) // UNIT
SUB_ROWS = SUB_UNITS * UNIT
EXPERT_BLOCK = 256
BLOCK_UNITS = EXPERT_BLOCK // UNIT
VMEM_LIMIT = 56 * 1024 * 1024


def _cparams(*sem):
    return pltpu.CompilerParams(dimension_semantics=sem, vmem_limit_bytes=VMEM_LIMIT)


def _dot(a, b):
    return jnp.dot(a, b, preferred_element_type=F32)


def _dot_nt(a, b):
    return lax.dot_general(a, b, (((1,), (1,)), ((), ())), preferred_element_type=F32)


def _norm_mod(x, g, shift, scale):
    y = x * lax.rsqrt(jnp.mean(x * x, axis=-1, keepdims=True) + EPS) * g
    return y * (1.0 + scale) + shift


def _rope128(z, cos, sin):
    lane = lax.broadcasted_iota(I32, (1, LANES), 1)
    partner = jnp.where((lane % 32) < 16, pltpu.roll(z, LANES - 16, 1), pltpu.roll(z, 16, 1))
    return z * cos + partner * sin


def _ada_body(c_ref, w_ref, b_ref, o_ref):
    c = c_ref[...]
    s = (c * jax.nn.sigmoid(c)).astype(BF16)
    o_ref[0] = _dot(s, w_ref[0].astype(BF16)) + b_ref[0]


def _ada_mod(cc, ada_w, ada_b):
    depth, d, n = ada_w.shape
    tn = 1024
    return pl.pallas_call(
        _ada_body,
        grid=(depth, n // tn),
        in_specs=[pl.BlockSpec((UNIT, d), lambda i, j: (0, 0)),
                  pl.BlockSpec((1, d, tn), lambda i, j: (i, 0, j)),
                  pl.BlockSpec((1, 1, tn), lambda i, j: (i, 0, j))],
        out_specs=pl.BlockSpec((1, UNIT, tn), lambda i, j: (i, 0, j)),
        out_shape=jax.ShapeDtypeStruct((depth, UNIT, n), F32),
        compiler_params=_cparams("parallel", "parallel"),
        name="ada_mod",
    )(cc, ada_w, ada_b.reshape(depth, 1, n))


class _Dims:
    def __init__(self, B, L, C, D):
        self.B, self.L, self.C, self.D = B, L, C, D
        self.n_lat = B * L
        self.T = B * L + B * C

    def mod_row(self, tm):
        return lambda i: jnp.minimum((i * tm) // self.L, self.B)


def _mod_spec(dims, tm, chunk, d):
    row = dims.mod_row(tm)
    return pl.BlockSpec((1, 1, d), lambda i: (row(i) * 6 + chunk, 0, 0))


def _row_spec(tm, n):
    return pl.BlockSpec((tm, n), lambda i: (i, 0))


def _const_spec(shape):
    return pl.BlockSpec(shape, lambda i: (0,) * len(shape))


def _gqa_qkv_body(x_ref, g_ref, sh_ref, sc_ref, w_ref, b_ref, wvt_ref, bv_ref, cos_ref, sin_ref,
                  q_ref, k_ref, vt_ref):
    h = _norm_mod(x_ref[...], g_ref[...], sh_ref[0], sc_ref[0]).astype(BF16)
    cos, sin = cos_ref[...], sin_ref[...]
    tm = h.shape[0]
    nq = GQA_HEADS * GQA_HEAD_DIM
    nkv = GQA_KV_HEADS * GQA_HEAD_DIM
    dh = GQA_HEAD_DIM
    scale = dh ** -0.5
    for j in range(nq // MXU_COLS):
        z = _dot(h, w_ref[:, j * MXU_COLS:(j + 1) * MXU_COLS]) + b_ref[:, j * MXU_COLS:(j + 1) * MXU_COLS]
        for c in range(2):
            r = _rope128(z[:, c * LANES:(c + 1) * LANES], cos, sin) * scale
            q_ref[:, j * MXU_COLS + c * LANES:j * MXU_COLS + (c + 1) * LANES] = r.astype(BF16)
    zk = _dot(h, w_ref[:, nq:nq + nkv]) + b_ref[:, nq:nq + nkv]
    for c in range(nkv // LANES):
        r = _rope128(zk[:, c * LANES:(c + 1) * LANES], cos, sin)
        k_ref[2 * c] = r[:, :dh].astype(BF16)
        k_ref[2 * c + 1] = r[:, dh:].astype(BF16)
    zvt = _dot_nt(wvt_ref[...], h) + bv_ref[...]
    ones_rows = jnp.where(lax.broadcasted_iota(I32, (GQA_VT_ROWS - dh, tm), 0) == 0, 1.0, 0.0).astype(BF16)
    for kv in range(GQA_KV_HEADS):
        vt_ref[kv, :dh, :] = zvt[kv * dh:(kv + 1) * dh, :].astype(BF16)
        vt_ref[kv, dh:, :] = ones_rows


def _gqa_qkv(dims, x, g, mod, w_qk, b_qk, wvt, bv, cos, sin, tm):
    T, D = dims.T, dims.D
    n = w_qk.shape[1]
    nq = GQA_HEADS * GQA_HEAD_DIM
    nkv = GQA_KV_HEADS * GQA_HEAD_DIM
    return pl.pallas_call(
        _gqa_qkv_body,
        grid=(T // tm,),
        in_specs=[_row_spec(tm, D), _const_spec((1, D)), _mod_spec(dims, tm, 0, D), _mod_spec(dims, tm, 1, D),
                  _const_spec((D, n)), _const_spec((1, n)), _const_spec((nkv, D)), _const_spec((nkv, 1)),
                  _row_spec(tm, LANES), _row_spec(tm, LANES)],
        out_specs=[_row_spec(tm, nq),
                   pl.BlockSpec((GQA_KV_HEADS, tm, GQA_HEAD_DIM), lambda i: (0, i, 0)),
                   pl.BlockSpec((GQA_KV_HEADS, GQA_VT_ROWS, tm), lambda i: (0, 0, i))],
        out_shape=[jax.ShapeDtypeStruct((T, nq), BF16),
                   jax.ShapeDtypeStruct((GQA_KV_HEADS, T, GQA_HEAD_DIM), BF16),
                   jax.ShapeDtypeStruct((GQA_KV_HEADS, GQA_VT_ROWS, T), BF16)],
        compiler_params=_cparams("parallel"),
        name="gqa_qkv",
    )(x, g, mod, mod, w_qk, b_qk, wvt, bv, cos, sin)


def _gqa_attn_body(sink_ref, q_ref, kp_ref, km_ref, kn_ref, kc_ref, vp_ref, vm_ref, vn_ref, vc_ref, o_ref, ot_ref,
                   *, tq, L, local):
    n = pl.program_id(1)
    kv = pl.program_id(2)
    dh = GQA_HEAD_DIM
    kc, vtc = kc_ref[0], vc_ref[0]
    if local:
        kl = jnp.concatenate([kp_ref[0], km_ref[0], kn_ref[0]], axis=0)
        vtl = jnp.concatenate([vp_ref[0], vm_ref[0], vn_ref[0]], axis=1)
        nk = tq + 2 * WINDOW
        rel = lax.broadcasted_iota(I32, (nk, tq), 0) - lax.broadcasted_iota(I32, (nk, tq), 1)
        pos = n * tq - WINDOW + lax.broadcasted_iota(I32, (nk, tq), 0)
        valid = (rel >= 0) & (rel <= 2 * WINDOW) & (pos >= 0) & (pos < L)

    def scores(gq):
        qh = q_ref[:, gq * dh:(gq + 1) * dh]
        return _dot_nt(kc, qh), (_dot_nt(kl, qh) if local else None)

    ahead = [scores(gq) for gq in range(GQA_LOOKAHEAD)]
    for gq in range(GQA_GROUP):
        s_ctx, s_loc = ahead.pop(0)
        if gq + GQA_LOOKAHEAD < GQA_GROUP:
            ahead.append(scores(gq + GQA_LOOKAHEAD))
        sk = sink_ref[kv * GQA_GROUP + gq]
        m = jnp.maximum(jnp.max(s_ctx, axis=0, keepdims=True), sk)
        if local:
            s_loc = jnp.where(valid, s_loc, -jnp.inf)
            m = jnp.maximum(m, jnp.max(s_loc, axis=0, keepdims=True))
        acc = _dot(vtc, jnp.exp((s_ctx - m).astype(BF16)))
        if local:
            acc = acc + _dot(vtl, jnp.exp((s_loc - m).astype(BF16)))
        den = acc[dh:dh + 1] + jnp.exp(sk - m)
        ot_ref[gq * dh:(gq + 1) * dh, :] = acc[:dh] / den
    o_ref[...] = ot_ref[...].T.astype(BF16)


def _gqa_attn(dims, q, k, vt, sink, *, local, tq):
    B, L, C, T = dims.B, dims.L, dims.C, dims.T
    gw = GQA_GROUP * GQA_HEAD_DIM
    hb = WINDOW
    if local:
        nq, row0, n_out = L // tq, lambda b, n: b * (L // tq) + n, B * L
    else:
        nq, row0, n_out = 1, lambda b, n: (B * L) // tq + b, B * C
    out_row0 = (lambda b, n: row0(b, n)) if local else (lambda b, n: b)
    last_hb = T // hb - 1

    def q_map(b, n, kv, s):
        return (row0(b, n), kv)

    def o_map(b, n, kv, s):
        return (out_row0(b, n), kv)

    def main_blk(b, n):
        return row0(b, n)

    def prev_blk(b, n):
        return jnp.maximum(row0(b, n) * (tq // hb) - 1, 0)

    def next_blk(b, n):
        return jnp.minimum((row0(b, n) + 1) * (tq // hb), last_hb)

    def ctx_blk(b, n):
        return (B * L) // C + b

    def k_spec(rows, blk):
        return pl.BlockSpec((1, rows, GQA_HEAD_DIM), lambda b, n, kv, s: (kv, blk(b, n), 0))

    def vt_spec(cols, blk):
        return pl.BlockSpec((1, GQA_VT_ROWS, cols), lambda b, n, kv, s: (kv, 0, blk(b, n)))

    blocks = [(hb, prev_blk), (tq, main_blk), (hb, next_blk), (C, ctx_blk)]
    grid_spec = pltpu.PrefetchScalarGridSpec(
        num_scalar_prefetch=1,
        grid=(B, nq, GQA_KV_HEADS),
        in_specs=[pl.BlockSpec((tq, gw), q_map)] + [k_spec(*x) for x in blocks] + [vt_spec(*x) for x in blocks],
        out_specs=pl.BlockSpec((tq, gw), o_map),
        scratch_shapes=[pltpu.VMEM((gw, tq), F32)],
    )
    return pl.pallas_call(
        functools.partial(_gqa_attn_body, tq=tq, L=L, local=local),
        grid_spec=grid_spec, out_shape=jax.ShapeDtypeStruct((n_out, q.shape[1]), BF16),
        compiler_params=_cparams("parallel", "parallel", "parallel"),
        name="gqa_attn_local" if local else "gqa_attn_ctx",
    )(sink, q, k, k, k, k, vt, vt, vt, vt)


def _proj_res_body(*refs, tn, n_lat_tiles, with_ctx):
    if with_ctx:
        al_ref, ac_ref, w_ref, b_ref, x_ref, gate_ref, o_ref = refs
    else:
        al_ref, w_ref, b_ref, x_ref, gate_ref, o_ref = refs
    gate = gate_ref[0]

    def run(a_ref):
        a = a_ref[...]
        for j in range(o_ref.shape[1] // tn):
            sl = slice(j * tn, (j + 1) * tn)
            y = _dot(a, w_ref[:, sl]) + b_ref[:, sl]
            o_ref[:, sl] = x_ref[:, sl] + gate[:, sl] * y

    if with_ctx:
        is_lat = pl.program_id(0) < n_lat_tiles
        pl.when(is_lat)(lambda: run(al_ref))
        pl.when(jnp.logical_not(is_lat))(lambda: run(ac_ref))
    else:
        run(al_ref)


def _proj_res(dims, a_lat, a_ctx, w, b, x, mod, chunk, tm):
    T, D = dims.T, dims.D
    K = a_lat.shape[1]
    n_lat_tiles = dims.n_lat // tm
    with_ctx = a_ctx is not None
    a_specs = [pl.BlockSpec((tm, K), lambda i: (jnp.minimum(i, n_lat_tiles - 1), 0))]
    args = [a_lat]
    if with_ctx:
        a_specs.append(pl.BlockSpec((tm, K), lambda i: (jnp.maximum(i - n_lat_tiles, 0), 0)))
        args.append(a_ctx)
    return pl.pallas_call(
        functools.partial(_proj_res_body, tn=512, n_lat_tiles=n_lat_tiles, with_ctx=with_ctx),
        grid=(T // tm if with_ctx else n_lat_tiles,),
        in_specs=a_specs + [_const_spec((K, D)), _const_spec((1, D)), _row_spec(tm, D),
                            _mod_spec(dims, tm, chunk, D)],
        out_specs=_row_spec(tm, D),
        out_shape=jax.ShapeDtypeStruct((T, D), F32),
        input_output_aliases={len(args) + 2: 0},
        compiler_params=_cparams("parallel"),
        name="proj_residual",
    )(*args, w, b, x, mod)


def _pool_body(xp_ref, x_ref, xn_ref, g_ref, sh_ref, sc_ref, gate_ref, w_ref, b_ref, ps_ref, o_ref, hf_ref,
               *, tm, dims):
    i = pl.program_id(0)
    row0 = i * tm
    is_lat = row0 < dims.n_lat
    n = jnp.where(is_lat, dims.L, dims.C)
    pos0 = jnp.where(is_lat, row0 % dims.L, (row0 - dims.n_lat) % dims.C)
    g, sh, sc = g_ref[...], sh_ref[0], sc_ref[0]
    x = x_ref[...]
    h = _norm_mod(x, g, sh, sc)
    hp = _norm_mod(xp_ref[...], g, sh, sc)
    hn = _norm_mod(xn_ref[...], g, sh, sc)
    hf_ref[0:POOL_HALO, :] = jnp.where(pos0 > 0, hp, 0.0)
    hf_ref[POOL_HALO:POOL_HALO + tm, :] = h
    hf_ref[POOL_HALO + tm:, :] = jnp.where(pos0 + tm < n, hn, 0.0)
    pos = pos0 + lax.broadcasted_iota(I32, (tm, 1), 0)
    gate = gate_ref[0]
    gw = dims.D // len(POOL_WINDOWS)
    for gi, win in enumerate(POOL_WINDOWS):
        half = win // 2
        sl = slice(gi * gw, (gi + 1) * gw)
        acc = hf_ref[POOL_HALO - half:POOL_HALO - half + tm, sl]
        for o in range(-half + 1, half):
            acc = acc + hf_ref[POOL_HALO + o:POOL_HALO + o + tm, sl]
        cnt = (jnp.minimum(pos + half, n) - jnp.maximum(pos - half, 0)).astype(F32)
        d = (acc / cnt - h[:, sl]).astype(BF16)
        y = (_dot(d, w_ref[gi]) + b_ref[:, sl]) * ps_ref[:, sl]
        o_ref[:, sl] = x[:, sl] + gate[:, sl] * y


def _pool(dims, x, g, mod, w, b, ps, tm):
    T, D = dims.T, dims.D
    hb = tm // POOL_HALO
    last = T // POOL_HALO - 1
    gw = D // len(POOL_WINDOWS)
    return pl.pallas_call(
        functools.partial(_pool_body, tm=tm, dims=dims),
        grid=(T // tm,),
        in_specs=[pl.BlockSpec((POOL_HALO, D), lambda i: (jnp.maximum(i * hb - 1, 0), 0)),
                  _row_spec(tm, D),
                  pl.BlockSpec((POOL_HALO, D), lambda i: (jnp.minimum((i + 1) * hb, last), 0)),
                  _const_spec((1, D)), _mod_spec(dims, tm, 0, D), _mod_spec(dims, tm, 1, D),
                  _mod_spec(dims, tm, 2, D), _const_spec((len(POOL_WINDOWS), gw, gw)),
                  _const_spec((1, D)), _const_spec((1, D))],
        out_specs=_row_spec(tm, D),
        out_shape=jax.ShapeDtypeStruct((T, D), F32),
        scratch_shapes=[pltpu.VMEM((tm + 2 * POOL_HALO, D), F32)],
        compiler_params=_cparams("parallel"),
        name="pool_mixer",
    )(x, x, x, g, mod, mod, mod, w, b, ps)


def _mla_down_body(x_ref, g_ref, sh_ref, sc_ref, w_ref, gq_ref, gkv_ref, cos_ref, sin_ref,
                   cq_ref, ckv_ref, kr_ref):
    h = _norm_mod(x_ref[...], g_ref[...], sh_ref[0], sc_ref[0]).astype(BF16)

    def rms(z, gain):
        return (z * lax.rsqrt(jnp.mean(z * z, axis=-1, keepdims=True) + EPS) * gain).astype(BF16)

    cq_ref[...] = rms(_dot(h, w_ref[:, :MLA_Q_RANK]), gq_ref[...])
    ckv_ref[...] = rms(_dot(h, w_ref[:, MLA_Q_RANK:MLA_Q_RANK + MLA_KV_RANK]), gkv_ref[...])
    zr = _dot(h, w_ref[:, MLA_Q_RANK + MLA_KV_RANK:])
    kr_ref[...] = _rope128(zr, cos_ref[...], sin_ref[...])[:, :MLA_ROPE].astype(BF16)


def _mla_down(dims, x, g, mod, w_pad, gq, gkv, cos, sin, tm):
    T, D = dims.T, dims.D
    n = w_pad.shape[1]
    return pl.pallas_call(
        _mla_down_body,
        grid=(T // tm,),
        in_specs=[_row_spec(tm, D), _const_spec((1, D)), _mod_spec(dims, tm, 0, D), _mod_spec(dims, tm, 1, D),
                  _const_spec((D, n)), _const_spec((1, MLA_Q_RANK)), _const_spec((1, MLA_KV_RANK)),
                  _row_spec(tm, LANES), _row_spec(tm, LANES)],
        out_specs=[_row_spec(tm, MLA_Q_RANK), _row_spec(tm, MLA_KV_RANK), _row_spec(tm, MLA_ROPE)],
        out_shape=[jax.ShapeDtypeStruct((T, MLA_Q_RANK), BF16), jax.ShapeDtypeStruct((T, MLA_KV_RANK), BF16),
                   jax.ShapeDtypeStruct((T, MLA_ROPE), BF16)],
        compiler_params=_cparams("parallel"),
        name="mla_down",
    )(x, g, mod, mod, w_pad, gq, gkv, cos, sin)


def _mla_up_body(cq_ref, ckv_ref, kr_ref, cos_ref, sin_ref, wqn_ref, wqr_ref, wk_ref, wvt_ref,
                 q_ref, k_ref, vt_ref):
    cq, ckv, kr = cq_ref[...], ckv_ref[...], kr_ref[...]
    cos, sin = cos_ref[...], sin_ref[...]
    scale = MLA_QK ** -0.5
    H = MLA_HEADS
    tm = cq.shape[0]
    ones_rows = jnp.where(lax.broadcasted_iota(I32, (MLA_VT_ROWS - MLA_V, tm), 0) == 0, 1.0, 0.0).astype(BF16)
    for j in range(H // 2):
        zq = _dot(cq, wqn_ref[:, j * MXU_COLS:(j + 1) * MXU_COLS]) * scale
        zk = _dot(ckv, wk_ref[:, j * MXU_COLS:(j + 1) * MXU_COLS])
        for c in range(2):
            hh = 2 * j + c
            q_ref[hh, :, :MLA_NOPE] = zq[:, c * LANES:(c + 1) * LANES].astype(BF16)
            k_ref[hh, :, :MLA_NOPE] = zk[:, c * LANES:(c + 1) * LANES].astype(BF16)
            k_ref[hh, :, MLA_NOPE:] = kr
            vt_ref[hh, :MLA_V, :] = _dot_nt(wvt_ref[hh * MLA_V:(hh + 1) * MLA_V, :], ckv).astype(BF16)
            vt_ref[hh, MLA_V:, :] = ones_rows
    for j in range(H // 4):
        zr = _dot(cq, wqr_ref[:, j * MXU_COLS:(j + 1) * MXU_COLS])
        for c in range(2):
            r = (_rope128(zr[:, c * LANES:(c + 1) * LANES], cos, sin) * scale).astype(BF16)
            q_ref[4 * j + 2 * c, :, MLA_NOPE:] = r[:, :MLA_ROPE]
            q_ref[4 * j + 2 * c + 1, :, MLA_NOPE:] = r[:, MLA_ROPE:]


def _mla_up(dims, cq, ckv, kr, cos, sin, wqn, wqr, wk, wvt, tm):
    T = dims.T
    H = MLA_HEADS

    def hspec(n):
        return pl.BlockSpec((H, tm, n), lambda i: (0, i, 0))

    return pl.pallas_call(
        _mla_up_body,
        grid=(T // tm,),
        in_specs=[_row_spec(tm, MLA_Q_RANK), _row_spec(tm, MLA_KV_RANK), _row_spec(tm, MLA_ROPE),
                  _row_spec(tm, LANES), _row_spec(tm, LANES),
                  _const_spec(wqn.shape), _const_spec(wqr.shape), _const_spec(wk.shape), _const_spec(wvt.shape)],
        out_specs=[hspec(MLA_QK), hspec(MLA_QK), pl.BlockSpec((H, MLA_VT_ROWS, tm), lambda i: (0, 0, i))],
        out_shape=[jax.ShapeDtypeStruct((H, T, MLA_QK), BF16), jax.ShapeDtypeStruct((H, T, MLA_QK), BF16),
                   jax.ShapeDtypeStruct((H, MLA_VT_ROWS, T), BF16)],
        compiler_params=_cparams("parallel"),
        name="mla_up",
    )(cq, ckv, kr, cos, sin, wqn, wqr, wk, wvt)


def _mla_attn_body(*refs, tk, n_chunks, latent_keys):
    if latent_keys:
        q_ref, kc_ref, vc_ref, kl_ref, vl_ref, o_ref, acc_ref = refs
    else:
        q_ref, kc_ref, vc_ref, o_ref, acc_ref = refs
    q = q_ref[0]

    def scores(c):
        return _dot_nt(kl_ref[0, c * tk:(c + 1) * tk, :], q).astype(BF16)

    s = _dot_nt(kc_ref[0], q).astype(BF16)
    ahead = [scores(c) for c in range(min(MLA_LOOKAHEAD, n_chunks))] if latent_keys else []
    m = jnp.max(s, axis=0, keepdims=True)
    acc_ref[...] = _dot(vc_ref[0], jnp.exp(s - m))
    if latent_keys:
        for c in range(n_chunks):
            s = ahead.pop(0)
            if c + MLA_LOOKAHEAD < n_chunks:
                ahead.append(scores(c + MLA_LOOKAHEAD))
            m2 = jnp.maximum(m, jnp.max(s, axis=0, keepdims=True))
            rescale = jnp.exp(m - m2).astype(F32)
            acc_ref[...] = rescale * acc_ref[...] + _dot(vl_ref[0, :, c * tk:(c + 1) * tk], jnp.exp(s - m2))
            m = m2
    acc = acc_ref[...]
    o_ref[...] = (acc[:MLA_V] / acc[MLA_V:MLA_V + 1]).T.astype(BF16)


def _mla_attn(dims, q, k, vt, *, latent_queries, tq, tk):
    B, L, C = dims.B, dims.L, dims.C
    H = MLA_HEADS
    if latent_queries:
        nq, row0, n_out = L // tq, lambda b, n: b * (L // tq) + n, B * L
    else:
        nq, row0, n_out = 1, lambda b, n: (B * L) // tq + b, B * C
    out_row0 = row0 if latent_queries else (lambda b, n: b)

    def q_map(b, h, n):
        return (h, row0(b, n), 0)

    def o_map(b, h, n):
        return (out_row0(b, n), h)

    in_specs = [pl.BlockSpec((1, tq, MLA_QK), q_map),
                pl.BlockSpec((1, C, MLA_QK), lambda b, h, n: (h, (B * L) // C + b, 0)),
                pl.BlockSpec((1, MLA_VT_ROWS, C), lambda b, h, n: (h, 0, (B * L) // C + b))]
    args = [q, k, vt]
    if latent_queries:
        in_specs += [pl.BlockSpec((1, L, MLA_QK), lambda b, h, n: (h, b, 0)),
                     pl.BlockSpec((1, MLA_VT_ROWS, L), lambda b, h, n: (h, 0, b))]
        args += [k, vt]
    return pl.pallas_call(
        functools.partial(_mla_attn_body, tk=tk, n_chunks=L // tk, latent_keys=latent_queries),
        grid=(B, H, nq),
        in_specs=in_specs,
        out_specs=pl.BlockSpec((tq, MLA_V), o_map),
        out_shape=jax.ShapeDtypeStruct((n_out, H * MLA_V), BF16),
        scratch_shapes=[pltpu.VMEM((MLA_VT_ROWS, tq), F32)],
        compiler_params=_cparams("parallel", "parallel", "arbitrary"),
        name="mla_attn_latent" if latent_queries else "mla_attn_ctx",
    )(*args)


def _router_body(x_ref, g_ref, sh_ref, sc_ref, whi_ref, wlo_ref, br_ref, hn_ref, rho_ref, gate_ref, cnt_ref):
    h = _norm_mod(x_ref[...], g_ref[...], sh_ref[0], sc_ref[0])
    hi = h.astype(BF16)
    lo = (h - hi.astype(F32)).astype(BF16)
    hn_ref[...] = hi
    whi, wlo = whi_ref[...], wlo_ref[...]
    lt = _dot_nt(whi, hi) + _dot_nt(wlo, hi) + _dot_nt(whi, lo) + br_ref[...]
    E = N_EXPERTS
    eidx = lax.broadcasted_iota(I32, (E, SUB), 0)
    onehots, tops = [], []
    for _ in range(TOP_K):
        m = jnp.max(lt, axis=0, keepdims=True)
        idx = jnp.min(jnp.where(lt == m, eidx, E), axis=0, keepdims=True)
        oh = eidx == idx
        onehots.append(oh)
        tops.append(m)
        lt = jnp.where(oh, -jnp.inf, lt)
    ex = [jnp.exp(t - tops[0]) for t in tops]
    den = ex[0] + ex[1] + ex[2] + ex[3]
    chosen = (onehots[0] | onehots[1] | onehots[2] | onehots[3])
    chosen_f = jnp.where(chosen, 1.0, 0.0)
    cnt = jnp.sum(chosen_f, axis=1, keepdims=True)
    units = jnp.floor((cnt + (UNIT - 1)) / UNIT)
    cnt_ref[0] = units
    lower = (lax.broadcasted_iota(I32, (E, E), 1) < lax.broadcasted_iota(I32, (E, E), 0))
    start = _dot(jnp.where(lower, 1.0, 0.0).astype(BF16),
                 jnp.broadcast_to(units, (E, LANES)).astype(BF16))[:, :1] * UNIT
    before = (lax.broadcasted_iota(I32, (SUB, SUB), 0) < lax.broadcasted_iota(I32, (SUB, SUB), 1))
    rank = _dot(chosen_f.astype(BF16), jnp.where(before, 1.0, 0.0).astype(BF16))
    row = start + rank
    for k in range(TOP_K):
        rho_ref[k:k + 1, :] = jnp.sum(jnp.where(onehots[k], row, 0.0), axis=0, keepdims=True).astype(I32)
        gate_ref[k:k + 1, :] = ex[k] / den


def _router(dims, x, g, mod, whi, wlo, br, n_rows):
    T, D = n_rows, dims.D
    n_sub = T // SUB
    E = N_EXPERTS
    return pl.pallas_call(
        _router_body,
        grid=(n_sub,),
        in_specs=[_row_spec(SUB, D), _const_spec((1, D)), _mod_spec(dims, SUB, 3, D), _mod_spec(dims, SUB, 4, D),
                  _const_spec((E, D)), _const_spec((E, D)), _const_spec((E, 1))],
        out_specs=[_row_spec(SUB, D), pl.BlockSpec((TOP_K, SUB), lambda i: (0, i)),
                   pl.BlockSpec((TOP_K, SUB), lambda i: (0, i)), pl.BlockSpec((1, E, 1), lambda i: (i, 0, 0))],
        out_shape=[jax.ShapeDtypeStruct((T, D), BF16), jax.ShapeDtypeStruct((TOP_K, T), I32),
                   jax.ShapeDtypeStruct((TOP_K, T), F32), jax.ShapeDtypeStruct((n_sub, E, 1), F32)],
        compiler_params=_cparams("parallel"),
        name="moe_router",
    )(x, g, mod, mod, whi, wlo, br)


def _dispatch_body(hn_ref, rho_ref, a_ref, *, n_sub):
    u = pl.program_id(0)
    rows = lax.broadcasted_iota(I32, (SUB_ROWS, SUB), 0)
    hit = rows == rho_ref[0:1, :]
    for k in range(1, TOP_K):
        hit = hit | (rows == rho_ref[k:k + 1, :])
    sel = jnp.where(hit & (u < n_sub), 1.0, 0.0).astype(BF16)
    a_ref[...] = _dot(sel, hn_ref[...]).astype(BF16)


def _dispatch(dims, hn, rho):
    T, D = hn.shape
    n_sub = T // SUB
    last = n_sub - 1
    return pl.pallas_call(
        functools.partial(_dispatch_body, n_sub=n_sub),
        grid=(n_sub + 1,),
        in_specs=[pl.BlockSpec((SUB, D), lambda u: (jnp.minimum(u, last), 0)),
                  pl.BlockSpec((TOP_K, SUB), lambda u: (0, jnp.minimum(u, last)))],
        out_specs=pl.BlockSpec((SUB_ROWS, D), lambda u: (u, 0)),
        out_shape=jax.ShapeDtypeStruct(((n_sub + 1) * SUB_ROWS, D), BF16),
        compiler_params=_cparams("parallel"),
        name="moe_dispatch",
    )(hn, rho)


def _expert_body(src_ref, be_ref, nu_ref, *refs):
    xs = refs[:BLOCK_UNITS]
    w1_ref, b1_ref, w2_ref, b2_ref, y_ref = refs[BLOCK_UNITS:]
    j = pl.program_id(0)

    @pl.when(j < nu_ref[0])
    def _():
        x = jnp.concatenate([r[...] for r in xs], axis=0)
        a = _dot(x, w1_ref[0]) + b1_ref[0]
        a_glu = jnp.minimum(a[:, :EXPERT_FF], SWIGLU_LIMIT)
        a_lin = jnp.clip(a[:, EXPERT_FF:], -SWIGLU_LIMIT, SWIGLU_LIMIT)
        u = a_glu * jax.nn.sigmoid(SWIGLU_ALPHA * a_glu) * (a_lin + 1.0)
        y_ref[...] = (_dot(u.astype(BF16), w2_ref[0]) + b2_ref[0]).astype(BF16)

    @pl.when(j >= nu_ref[0])
    def _():
        y_ref[...] = jnp.zeros_like(y_ref)


def _experts(a, src_unit, blk_e, n_used, layer, w1, b1, w2, b2, n_blocks):
    D = a.shape[1]
    ff2 = w1.shape[2]

    def unit_map(i, j, src, be, nu):
        return (src[j * BLOCK_UNITS + i], 0)

    def e_map(j, src, be, nu):
        return (layer * N_EXPERTS + be[j], 0, 0)

    grid_spec = pltpu.PrefetchScalarGridSpec(
        num_scalar_prefetch=3,
        grid=(n_blocks,),
        in_specs=[pl.BlockSpec((UNIT, D), functools.partial(unit_map, i)) for i in range(BLOCK_UNITS)]
        + [pl.BlockSpec((1, D, ff2), e_map), pl.BlockSpec((1, 1, ff2), e_map),
           pl.BlockSpec((1, EXPERT_FF, D), e_map), pl.BlockSpec((1, 1, D), e_map)],
        out_specs=pl.BlockSpec((EXPERT_BLOCK, D), lambda j, src, be, nu: (j, 0)),
    )
    return pl.pallas_call(
        _expert_body,
        grid_spec=grid_spec,
        out_shape=jax.ShapeDtypeStruct((n_blocks * EXPERT_BLOCK, D), BF16),
        compiler_params=_cparams("arbitrary"),
        name="moe_experts",
    )(src_unit, blk_e, n_used, *([a] * BLOCK_UNITS), w1, b1, w2, b2)


def _combine_body(dst_ref, x_ref, rho_ref, gate_ref, g2_ref, *rest, n_sub, final):
    if final:
        fg_ref, ys_ref, o_ref, buf_ref, sem_ref = rest
    else:
        ys_ref, o_ref, buf_ref, sem_ref = rest
    u = pl.program_id(0)

    def start_all(uu, slot):
        for i in range(SUB_UNITS):
            d = jnp.maximum(dst_ref[uu * SUB_UNITS + i], 0)
            pltpu.make_async_copy(ys_ref.at[pl.ds(pl.multiple_of(d * UNIT, UNIT), UNIT)],
                                  buf_ref.at[slot, pl.ds(i * UNIT, UNIT)], sem_ref.at[slot]).start()

    def wait_slot(slot):
        pltpu.make_async_copy(ys_ref.at[pl.ds(0, SUB_ROWS)], buf_ref.at[slot], sem_ref.at[slot]).wait()

    @pl.when(u == 0)
    def _():
        start_all(0, 0)

    slot = u % 2
    wait_slot(slot)
    start_all(jnp.minimum(u + 1, n_sub - 1), (u + 1) % 2)
    cols = lax.broadcasted_iota(I32, (SUB, SUB_ROWS), 1)
    wgt = jnp.where(cols == rho_ref[:, 0:1], gate_ref[:, 0:1], 0.0)
    for k in range(1, TOP_K):
        wgt = wgt + jnp.where(cols == rho_ref[:, k:k + 1], gate_ref[:, k:k + 1], 0.0)
    o = x_ref[...] + g2_ref[0] * _dot(wgt.astype(BF16), buf_ref[slot])
    if final:
        o = o * lax.rsqrt(jnp.mean(o * o, axis=-1, keepdims=True) + EPS) * fg_ref[...]
    o_ref[...] = o

    @pl.when(u == n_sub - 1)
    def _():
        wait_slot((u + 1) % 2)


def _combine(dims, dst_unit, x, rho_t, gate_t, mod, ys, n_rows, final_g):
    D = dims.D
    n_sub = n_rows // SUB
    final = final_g is not None
    in_specs = [pl.BlockSpec((SUB, D), lambda u, dst: (u, 0)),
                pl.BlockSpec((SUB, TOP_K), lambda u, dst: (u, 0)),
                pl.BlockSpec((SUB, TOP_K), lambda u, dst: (u, 0)),
                pl.BlockSpec((1, 1, D), lambda u, dst: (dims.mod_row(SUB)(u) * 6 + 5, 0, 0))]
    args = [x, rho_t, gate_t, mod]
    if final:
        in_specs.append(pl.BlockSpec((1, D), lambda u, dst: (0, 0)))
        args.append(final_g)
    grid_spec = pltpu.PrefetchScalarGridSpec(
        num_scalar_prefetch=1,
        grid=(n_sub,),
        in_specs=in_specs + [pl.BlockSpec(memory_space=pl.ANY)],
        out_specs=pl.BlockSpec((SUB, D), lambda u, dst: (u, 0)),
        scratch_shapes=[pltpu.VMEM((2, SUB_ROWS, D), BF16), pltpu.SemaphoreType.DMA((2,))],
    )
    return pl.pallas_call(
        functools.partial(_combine_body, n_sub=n_sub, final=final),
        grid_spec=grid_spec,
        out_shape=jax.ShapeDtypeStruct((n_rows if final else x.shape[0], D), F32),
        input_output_aliases={} if final else {1: 0},
        compiler_params=_cparams("arbitrary"),
        name="moe_combine",
    )(dst_unit, *args, ys)


def _moe_tables(units, n_blocks):
    n_sub, E = units.shape
    g = units.astype(I32)
    off_a = jnp.cumsum(g, axis=1) - g
    tot = jnp.sum(g, axis=0)
    pad = (tot + BLOCK_UNITS - 1) // BLOCK_UNITS * BLOCK_UNITS
    pend = jnp.cumsum(pad)
    pos_x = (pend - pad)[None, :] + jnp.cumsum(g, axis=0) - g
    sub_base = jnp.arange(n_sub, dtype=I32)[:, None] * SUB_UNITS + off_a
    p = jnp.arange(n_blocks * BLOCK_UNITS, dtype=I32)[:, None]
    lo, cnt, delta = pos_x.reshape(1, -1), g.reshape(1, -1), (sub_base - pos_x).reshape(1, -1)
    inside = (p >= lo) & (p < lo + cnt)
    src_unit = jnp.where(jnp.any(inside, axis=1), jnp.sum(jnp.where(inside, delta + p, 0), axis=1),
                         n_sub * SUB_UNITS).astype(I32)
    blk_e = jnp.minimum(jnp.sum(jnp.arange(n_blocks, dtype=I32)[:, None] * BLOCK_UNITS >= pend[None, :], axis=1),
                        E - 1).astype(I32)
    n_used = (pend[-1:] // BLOCK_UNITS).astype(I32)
    i = jnp.arange(SUB_UNITS, dtype=I32)[None, :, None]
    inside = (i >= off_a[:, None, :]) & (i < (off_a + g)[:, None, :])
    dst = jnp.sum(jnp.where(inside, (pos_x - off_a)[:, None, :] + i, 0), axis=-1)
    dst_unit = jnp.where(jnp.any(inside, axis=-1), dst, -1).astype(I32).reshape(-1)
    return src_unit, blk_e, n_used, dst_unit


def _moe(dims, x, g, mod, whi, wlo, br, layer, w1, b1, w2, b2, n_rows, final_g):
    n_sub = n_rows // SUB
    n_blocks = (n_sub * SUB_UNITS + N_EXPERTS * (BLOCK_UNITS - 1)) // BLOCK_UNITS + 1
    hn, rho, gate, units = _router(dims, x, g, mod, whi, wlo, br, n_rows)
    src_unit, blk_e, n_used, dst_unit = _moe_tables(units.reshape(n_sub, N_EXPERTS), n_blocks)
    a = _dispatch(dims, hn, rho)
    ys = _experts(a, src_unit, blk_e, n_used, layer, w1, b1, w2, b2, n_blocks)
    return _combine(dims, dst_unit, x, rho.T, gate.T, mod, ys, n_rows, final_g)


def _rope_tables(B, L, C):
    half = GQA_HEAD_DIM // 4
    inv = ROPE_BASE ** (-jnp.arange(half, dtype=F32) / half)
    t = jnp.arange(L)
    ang_r = (t // GRID_W).astype(F32)[:, None] * inv[None, :]
    ang_c = (t % GRID_W).astype(F32)[:, None] * inv[None, :]
    cos = jnp.concatenate([jnp.cos(ang_r)] * 2 + [jnp.cos(ang_c)] * 2, axis=-1)
    sin = jnp.concatenate([-jnp.sin(ang_r), jnp.sin(ang_r), -jnp.sin(ang_c), jnp.sin(ang_c)], axis=-1)
    cos = jnp.tile(cos, (B, LANES // cos.shape[1]))
    sin = jnp.tile(sin, (B, LANES // sin.shape[1]))
    cos = jnp.concatenate([cos, jnp.ones((B * C, LANES), F32)], axis=0)
    sin = jnp.concatenate([sin, jnp.zeros((B * C, LANES), F32)], axis=0)
    return cos, sin


def kernel(x, c, ctx, c_ctx, ada_w, ada_b, norm_g, final_g, gqa_w_qkv, gqa_b_qkv, gqa_sink, gqa_w_o, gqa_b_o,
           pool_w, pool_b, pool_scale, mla_w_down, mla_g_q, mla_w_uq, mla_g_kv, mla_w_ukv, mla_w_o,
           router_w, router_b, exp_w1, exp_b1, exp_w2, exp_b2):
    B, L, D = x.shape
    C = ctx.shape[1]
    depth = ada_w.shape[0]
    dims = _Dims(B, L, C, D)
    T = dims.T
    tm = 512 if (L % 512 == 0 and (B * C) % 512 == 0) else 256
    assert L % tm == 0 and (B * C) % tm == 0 and L % SUB == 0 and C % SUB == 0 and B + 1 <= UNIT
    assert L % GRID_W == 0 and SUB % WINDOW == 0 and C % WINDOW == 0

    cc = jnp.zeros((UNIT, D), F32).at[:B].set(c).at[B].set(c_ctx)
    mods = _ada_mod(cc, ada_w, ada_b)
    cos, sin = _rope_tables(B, L, C)
    xs = jnp.concatenate([x.reshape(B * L, D), ctx.reshape(B * C, D)], axis=0)
    ne = depth * N_EXPERTS
    w1 = exp_w1.astype(BF16).reshape(ne, D, -1)
    w2 = exp_w2.astype(BF16).reshape(ne, -1, D)
    b1 = exp_b1.reshape(ne, 1, -1)
    b2 = exp_b2.reshape(ne, 1, D)

    for i in range(depth):
        need_ctx = i < depth - 1
        mod = mods[i].reshape(UNIT * 6, 1, D)
        kind, j = i % N_MIXERS, i // N_MIXERS
        g1 = norm_g[i, 0].reshape(1, D)
        if kind == 0:
            nqk = (GQA_HEADS + GQA_KV_HEADS) * GQA_HEAD_DIM
            q, k, vt = _gqa_qkv(dims, xs, g1, mod, gqa_w_qkv[j][:, :nqk].astype(BF16),
                                gqa_b_qkv[j][:nqk].reshape(1, -1), gqa_w_qkv[j][:, nqk:].T.astype(BF16),
                                gqa_b_qkv[j][nqk:].reshape(-1, 1), cos, sin, tm)
            o = _gqa_attn(dims, q, k, vt, gqa_sink[j], local=True, tq=SUB)
            oc = _gqa_attn(dims, q, k, vt, gqa_sink[j], local=False, tq=C) if need_ctx else None
            xs = _proj_res(dims, o, oc, gqa_w_o[j].astype(BF16), gqa_b_o[j].reshape(1, D), xs, mod, 2, tm)
        elif kind == 1:
            xs = _pool(dims, xs, g1, mod, pool_w[j].astype(BF16), pool_b[j].reshape(1, D),
                       pool_scale[j].reshape(1, D), SUB)
        else:
            H = MLA_HEADS
            w_down = jnp.pad(mla_w_down[j], ((0, 0), (0, LANES - MLA_ROPE))).astype(BF16)
            wq = mla_w_uq[j].reshape(MLA_Q_RANK, H, MLA_QK)
            wqn = wq[:, :, :MLA_NOPE].reshape(MLA_Q_RANK, H * MLA_NOPE).astype(BF16)
            wqr = wq[:, :, MLA_NOPE:].reshape(MLA_Q_RANK, H * MLA_ROPE).astype(BF16)
            wkv = mla_w_ukv[j].reshape(MLA_KV_RANK, H, MLA_NOPE + MLA_V)
            wk = wkv[:, :, :MLA_NOPE].reshape(MLA_KV_RANK, H * MLA_NOPE).astype(BF16)
            wvt = wkv[:, :, MLA_NOPE:].reshape(MLA_KV_RANK, H * MLA_V).T.astype(BF16)
            cq, ckv, kr = _mla_down(dims, xs, g1, mod, w_down, mla_g_q[j].reshape(1, -1),
                                    mla_g_kv[j].reshape(1, -1), cos, sin, tm)
            q, k, vt = _mla_up(dims, cq, ckv, kr, cos, sin, wqn, wqr, wk, wvt, SUB)
            o = _mla_attn(dims, q, k, vt, latent_queries=True, tq=tm, tk=min(1024, L))
            oc = _mla_attn(dims, q, k, vt, latent_queries=False, tq=C, tk=min(1024, L)) if need_ctx else None
            xs = _proj_res(dims, o, oc, mla_w_o[j].astype(BF16), jnp.zeros((1, D), F32), xs, mod, 2, tm)
        wr_t = router_w[i].T
        whi = wr_t.astype(BF16)
        wlo = (wr_t - whi.astype(F32)).astype(BF16)
        last = i == depth - 1
        xs = _moe(dims, xs, norm_g[i, 1].reshape(1, D), mod, whi, wlo, router_b[i].reshape(-1, 1), i, w1, b1, w2, b2,
                  B * L if last else T, final_g.reshape(1, D) if last else None)
    return xs.reshape(B, L, D)
```

```python
import functools

import jax
import jax.numpy as jnp
from jax import lax
from jax.experimental import pallas as pl
from jax.experimental.pallas import tpu as pltpu

F32 = jnp.float32
BF16 = jnp.bfloat16
I32 = jnp.int32

GRID_W = 64
N_MIXERS = 3
EPS = 1e-6
LOG2E = 1.4426950408889634
ROPE_BASE = 10000.0

GQA_HEADS = 32
GQA_KV_HEADS = 4
GQA_HEAD_DIM = 64
GQA_GROUP = GQA_HEADS // GQA_KV_HEADS
WINDOW = 128
GQA_LOOKAHEAD = 2
GQA_VT_ROWS = GQA_HEAD_DIM + 16

POOL_WINDOWS = (2, 4, 8, 16)
POOL_HALO = 8

MLA_HEADS = 16
MLA_Q_RANK = 512
MLA_KV_RANK = 512
MLA_NOPE = 128
MLA_ROPE = 64
MLA_V = 128
MLA_QK = MLA_NOPE + MLA_ROPE
MLA_VT_ROWS = MLA_V + 16
MLA_LOOKAHEAD = 2

N_EXPERTS = 32
TOP_K = 4
EXPERT_FF = 768
SWIGLU_LIMIT = 7.0
SWIGLU_ALPHA = 1.702

LANES = 128
MXU_COLS = 256
UNIT = 16
SUB = 256
SUB_UNITS = (SUB * TOP_K + N_EXPERTS * (UNIT - 1) + UNIT - 1) // UNIT
SUB_ROWS = SUB_UNITS * UNIT
EXPERT_BLOCK = 256
CAST_STEPS_MAX = 512
BLOCK_UNITS = EXPERT_BLOCK // UNIT
VMEM_LIMIT = 56 * 1024 * 1024


def _cparams(*sem):
    return pltpu.CompilerParams(dimension_semantics=sem, vmem_limit_bytes=VMEM_LIMIT)


def _dot(a, b):
    return jnp.dot(a, b, preferred_element_type=F32)


def _dot_nt(a, b):
    return lax.dot_general(a, b, (((1,), (1,)), ((), ())), preferred_element_type=F32)


def _norm_mod(x, g, shift, scale):
    y = x * lax.rsqrt(jnp.mean(x * x, axis=-1, keepdims=True) + EPS) * g
    return y * (1.0 + scale) + shift


def _rope128(z, cos, sin):
    lane = lax.broadcasted_iota(I32, (1, LANES), 1)
    partner = jnp.where((lane % 32) < 16, pltpu.roll(z, LANES - 16, 1), pltpu.roll(z, 16, 1))
    return z * cos + partner * sin


def _ada_body(c_ref, w_ref, b_ref, o_ref):
    c = c_ref[...]
    s = (c * jax.nn.sigmoid(c)).astype(BF16)
    o_ref[0] = _dot(s, w_ref[0].astype(BF16)) + b_ref[0]


def _ada_mod(cc, ada_w, ada_b):
    depth, d, n = ada_w.shape
    tn = 1024
    return pl.pallas_call(
        _ada_body,
        grid=(depth, n // tn),
        in_specs=[pl.BlockSpec((UNIT, d), lambda i, j: (0, 0)),
                  pl.BlockSpec((1, d, tn), lambda i, j: (i, 0, j)),
                  pl.BlockSpec((1, 1, tn), lambda i, j: (i, 0, j))],
        out_specs=pl.BlockSpec((1, UNIT, tn), lambda i, j: (i, 0, j)),
        out_shape=jax.ShapeDtypeStruct((depth, UNIT, n), F32),
        compiler_params=_cparams("parallel", "parallel"),
        name="ada_mod",
    )(cc, ada_w, ada_b.reshape(depth, 1, n))


class _Dims:
    def __init__(self, B, L, C, D):
        self.B, self.L, self.C, self.D = B, L, C, D
        self.n_lat = B * L
        self.T = B * L + B * C

    def mod_row(self, tm):
        return lambda i: jnp.minimum((i * tm) // self.L, self.B)


def _mod_spec(dims, tm, chunk, d):
    row = dims.mod_row(tm)
    return pl.BlockSpec((1, 1, d), lambda i: (row(i) * 6 + chunk, 0, 0))


def _row_spec(tm, n):
    return pl.BlockSpec((tm, n), lambda i: (i, 0))


def _const_spec(shape):
    return pl.BlockSpec(shape, lambda i: (0,) * len(shape))


def _gqa_qkv_body(x_ref, g_ref, sh_ref, sc_ref, w_ref, b_ref, wvt_ref, bv_ref, cos_ref, sin_ref,
                  q_ref, k_ref, vt_ref):
    h = _norm_mod(x_ref[...], g_ref[...], sh_ref[0], sc_ref[0]).astype(BF16)
    cos, sin = cos_ref[...], sin_ref[...]
    tm = h.shape[0]
    nq = GQA_HEADS * GQA_HEAD_DIM
    nkv = GQA_KV_HEADS * GQA_HEAD_DIM
    dh = GQA_HEAD_DIM
    scale = dh ** -0.5 * LOG2E
    for j in range(nq // MXU_COLS):
        z = _dot(h, w_ref[:, j * MXU_COLS:(j + 1) * MXU_COLS]) + b_ref[:, j * MXU_COLS:(j + 1) * MXU_COLS]
        for c in range(2):
            r = _rope128(z[:, c * LANES:(c + 1) * LANES], cos, sin) * scale
            q_ref[:, j * MXU_COLS + c * LANES:j * MXU_COLS + (c + 1) * LANES] = r.astype(BF16)
    zk = _dot(h, w_ref[:, nq:nq + nkv]) + b_ref[:, nq:nq + nkv]
    for c in range(nkv // LANES):
        r = _rope128(zk[:, c * LANES:(c + 1) * LANES], cos, sin)
        k_ref[2 * c] = r[:, :dh].astype(BF16)
        k_ref[2 * c + 1] = r[:, dh:].astype(BF16)
    zvt = _dot_nt(wvt_ref[...], h) + bv_ref[...]
    ones_rows = jnp.where(lax.broadcasted_iota(I32, (GQA_VT_ROWS - dh, tm), 0) == 0, 1.0, 0.0).astype(BF16)
    for kv in range(GQA_KV_HEADS):
        vt_ref[kv, :dh, :] = zvt[kv * dh:(kv + 1) * dh, :].astype(BF16)
        vt_ref[kv, dh:, :] = ones_rows


def _gqa_qkv(dims, x, g, mod, w_qk, b_qk, wvt, bv, cos, sin, tm):
    T, D = dims.T, dims.D
    n = w_qk.shape[1]
    nq = GQA_HEADS * GQA_HEAD_DIM
    nkv = GQA_KV_HEADS * GQA_HEAD_DIM
    return pl.pallas_call(
        _gqa_qkv_body,
        grid=(T // tm,),
        in_specs=[_row_spec(tm, D), _const_spec((1, D)), _mod_spec(dims, tm, 0, D), _mod_spec(dims, tm, 1, D),
                  _const_spec((D, n)), _const_spec((1, n)), _const_spec((nkv, D)), _const_spec((nkv, 1)),
                  _row_spec(tm, LANES), _row_spec(tm, LANES)],
        out_specs=[_row_spec(tm, nq),
                   pl.BlockSpec((GQA_KV_HEADS, tm, GQA_HEAD_DIM), lambda i: (0, i, 0)),
                   pl.BlockSpec((GQA_KV_HEADS, GQA_VT_ROWS, tm), lambda i: (0, 0, i))],
        out_shape=[jax.ShapeDtypeStruct((T, nq), BF16),
                   jax.ShapeDtypeStruct((GQA_KV_HEADS, T, GQA_HEAD_DIM), BF16),
                   jax.ShapeDtypeStruct((GQA_KV_HEADS, GQA_VT_ROWS, T), BF16)],
        compiler_params=_cparams("parallel"),
        name="gqa_qkv",
    )(x, g, mod, mod, w_qk, b_qk, wvt, bv, cos, sin)


def _gqa_attn_body(sink_ref, q_ref, kp_ref, km_ref, kn_ref, kc_ref, vp_ref, vm_ref, vn_ref, vc_ref, o_ref, ot_ref,
                   *, tq, L, local):
    n = pl.program_id(1)
    kv = pl.program_id(2)
    dh = GQA_HEAD_DIM
    kc, vtc = kc_ref[0], vc_ref[0]
    if local:
        kl = jnp.concatenate([kp_ref[0], km_ref[0], kn_ref[0]], axis=0)
        vtl = jnp.concatenate([vp_ref[0], vm_ref[0], vn_ref[0]], axis=1)
        nk = tq + 2 * WINDOW
        rel = lax.broadcasted_iota(I32, (nk, tq), 0) - lax.broadcasted_iota(I32, (nk, tq), 1)
        pos = n * tq - WINDOW + lax.broadcasted_iota(I32, (nk, tq), 0)
        valid = (rel >= 0) & (rel <= 2 * WINDOW) & (pos >= 0) & (pos < L)

    def scores(gq):
        qh = q_ref[:, gq * dh:(gq + 1) * dh]
        return _dot_nt(kc, qh), (_dot_nt(kl, qh) if local else None)

    ahead = [scores(gq) for gq in range(GQA_LOOKAHEAD)]
    for gq in range(GQA_GROUP):
        s_ctx, s_loc = ahead.pop(0)
        if gq + GQA_LOOKAHEAD < GQA_GROUP:
            ahead.append(scores(gq + GQA_LOOKAHEAD))
        sk = sink_ref[kv * GQA_GROUP + gq] * LOG2E
        m = jnp.maximum(jnp.max(s_ctx, axis=0, keepdims=True), sk)
        if local:
            s_loc = jnp.where(valid, s_loc, -jnp.inf)
            m = jnp.maximum(m, jnp.max(s_loc, axis=0, keepdims=True))
        acc = _dot(vtc, jnp.exp2((s_ctx - m).astype(BF16)))
        if local:
            acc = acc + _dot(vtl, jnp.exp2((s_loc - m).astype(BF16)))
        den = acc[dh:dh + 1] + jnp.exp2(sk - m)
        ot_ref[gq * dh:(gq + 1) * dh, :] = acc[:dh] / den
    o_ref[...] = ot_ref[...].T.astype(BF16)


def _gqa_attn(dims, q, k, vt, sink, *, local, tq):
    B, L, C, T = dims.B, dims.L, dims.C, dims.T
    gw = GQA_GROUP * GQA_HEAD_DIM
    hb = WINDOW
    if local:
        nq, row0, n_out = L // tq, lambda b, n: b * (L // tq) + n, B * L
    else:
        nq, row0, n_out = 1, lambda b, n: (B * L) // tq + b, B * C
    out_row0 = (lambda b, n: row0(b, n)) if local else (lambda b, n: b)
    last_hb = T // hb - 1

    def q_map(b, n, kv, s):
        return (row0(b, n), kv)

    def o_map(b, n, kv, s):
        return (out_row0(b, n), kv)

    def main_blk(b, n):
        return row0(b, n)

    def prev_blk(b, n):
        return jnp.maximum(row0(b, n) * (tq // hb) - 1, 0)

    def next_blk(b, n):
        return jnp.minimum((row0(b, n) + 1) * (tq // hb), last_hb)

    def ctx_blk(b, n):
        return (B * L) // C + b

    def k_spec(rows, blk):
        return pl.BlockSpec((1, rows, GQA_HEAD_DIM), lambda b, n, kv, s: (kv, blk(b, n), 0))

    def vt_spec(cols, blk):
        return pl.BlockSpec((1, GQA_VT_ROWS, cols), lambda b, n, kv, s: (kv, 0, blk(b, n)))

    blocks = [(hb, prev_blk), (tq, main_blk), (hb, next_blk), (C, ctx_blk)]
    grid_spec = pltpu.PrefetchScalarGridSpec(
        num_scalar_prefetch=1,
        grid=(B, nq, GQA_KV_HEADS),
        in_specs=[pl.BlockSpec((tq, gw), q_map)] + [k_spec(*x) for x in blocks] + [vt_spec(*x) for x in blocks],
        out_specs=pl.BlockSpec((tq, gw), o_map),
        scratch_shapes=[pltpu.VMEM((gw, tq), F32)],
    )
    return pl.pallas_call(
        functools.partial(_gqa_attn_body, tq=tq, L=L, local=local),
        grid_spec=grid_spec, out_shape=jax.ShapeDtypeStruct((n_out, q.shape[1]), BF16),
        compiler_params=_cparams("parallel", "parallel", "parallel"),
        name="gqa_attn_local" if local else "gqa_attn_ctx",
    )(sink, q, k, k, k, k, vt, vt, vt, vt)


def _proj_res_body(*refs, tn, n_lat_tiles, with_ctx):
    if with_ctx:
        al_ref, ac_ref, w_ref, b_ref, x_ref, gate_ref, o_ref = refs
    else:
        al_ref, w_ref, b_ref, x_ref, gate_ref, o_ref = refs
    gate = gate_ref[0]

    def run(a_ref):
        a = a_ref[...]
        for j in range(o_ref.shape[1] // tn):
            sl = slice(j * tn, (j + 1) * tn)
            y = _dot(a, w_ref[:, sl]) + b_ref[:, sl]
            o_ref[:, sl] = x_ref[:, sl] + gate[:, sl] * y

    if with_ctx:
        is_lat = pl.program_id(0) < n_lat_tiles
        pl.when(is_lat)(lambda: run(al_ref))
        pl.when(jnp.logical_not(is_lat))(lambda: run(ac_ref))
    else:
        run(al_ref)


def _proj_res(dims, a_lat, a_ctx, w, b, x, mod, chunk, tm):
    T, D = dims.T, dims.D
    K = a_lat.shape[1]
    n_lat_tiles = dims.n_lat // tm
    with_ctx = a_ctx is not None
    a_specs = [pl.BlockSpec((tm, K), lambda i: (jnp.minimum(i, n_lat_tiles - 1), 0))]
    args = [a_lat]
    if with_ctx:
        a_specs.append(pl.BlockSpec((tm, K), lambda i: (jnp.maximum(i - n_lat_tiles, 0), 0)))
        args.append(a_ctx)
    return pl.pallas_call(
        functools.partial(_proj_res_body, tn=512, n_lat_tiles=n_lat_tiles, with_ctx=with_ctx),
        grid=(T // tm if with_ctx else n_lat_tiles,),
        in_specs=a_specs + [_const_spec((K, D)), _const_spec((1, D)), _row_spec(tm, D),
                            _mod_spec(dims, tm, chunk, D)],
        out_specs=_row_spec(tm, D),
        out_shape=jax.ShapeDtypeStruct((T, D), F32),
        input_output_aliases={len(args) + 2: 0},
        compiler_params=_cparams("parallel"),
        name="proj_residual",
    )(*args, w, b, x, mod)


def _pool_body(xp_ref, x_ref, xn_ref, g_ref, sh_ref, sc_ref, gate_ref, w_ref, b_ref, ps_ref, o_ref, hf_ref,
               *, tm, dims):
    i = pl.program_id(0)
    row0 = i * tm
    is_lat = row0 < dims.n_lat
    n = jnp.where(is_lat, dims.L, dims.C)
    pos0 = jnp.where(is_lat, row0 % dims.L, (row0 - dims.n_lat) % dims.C)
    g, sh, sc = g_ref[...], sh_ref[0], sc_ref[0]
    x = x_ref[...]
    h = _norm_mod(x, g, sh, sc)
    hp = _norm_mod(xp_ref[...], g, sh, sc)
    hn = _norm_mod(xn_ref[...], g, sh, sc)
    hf_ref[0:POOL_HALO, :] = jnp.where(pos0 > 0, hp, 0.0)
    hf_ref[POOL_HALO:POOL_HALO + tm, :] = h
    hf_ref[POOL_HALO + tm:, :] = jnp.where(pos0 + tm < n, hn, 0.0)
    pos = pos0 + lax.broadcasted_iota(I32, (tm, 1), 0)
    gate = gate_ref[0]
    gw = dims.D // len(POOL_WINDOWS)
    for gi, win in enumerate(POOL_WINDOWS):
        half = win // 2
        sl = slice(gi * gw, (gi + 1) * gw)
        acc = hf_ref[POOL_HALO - half:POOL_HALO - half + tm, sl]
        for o in range(-half + 1, half):
            acc = acc + hf_ref[POOL_HALO + o:POOL_HALO + o + tm, sl]
        cnt = (jnp.minimum(pos + half, n) - jnp.maximum(pos - half, 0)).astype(F32)
        d = (acc / cnt - h[:, sl]).astype(BF16)
        y = (_dot(d, w_ref[gi]) + b_ref[:, sl]) * ps_ref[:, sl]
        o_ref[:, sl] = x[:, sl] + gate[:, sl] * y


def _pool(dims, x, g, mod, w, b, ps, tm):
    T, D = dims.T, dims.D
    hb = tm // POOL_HALO
    last = T // POOL_HALO - 1
    gw = D // len(POOL_WINDOWS)
    return pl.pallas_call(
        functools.partial(_pool_body, tm=tm, dims=dims),
        grid=(T // tm,),
        in_specs=[pl.BlockSpec((POOL_HALO, D), lambda i: (jnp.maximum(i * hb - 1, 0), 0)),
                  _row_spec(tm, D),
                  pl.BlockSpec((POOL_HALO, D), lambda i: (jnp.minimum((i + 1) * hb, last), 0)),
                  _const_spec((1, D)), _mod_spec(dims, tm, 0, D), _mod_spec(dims, tm, 1, D),
                  _mod_spec(dims, tm, 2, D), _const_spec((len(POOL_WINDOWS), gw, gw)),
                  _const_spec((1, D)), _const_spec((1, D))],
        out_specs=_row_spec(tm, D),
        out_shape=jax.ShapeDtypeStruct((T, D), F32),
        scratch_shapes=[pltpu.VMEM((tm + 2 * POOL_HALO, D), F32)],
        compiler_params=_cparams("parallel"),
        name="pool_mixer",
    )(x, x, x, g, mod, mod, mod, w, b, ps)


def _mla_down_body(x_ref, g_ref, sh_ref, sc_ref, w_ref, gq_ref, gkv_ref, cos_ref, sin_ref,
                   cq_ref, ckv_ref, kr_ref):
    h = _norm_mod(x_ref[...], g_ref[...], sh_ref[0], sc_ref[0]).astype(BF16)

    def rms(z, gain):
        return (z * lax.rsqrt(jnp.mean(z * z, axis=-1, keepdims=True) + EPS) * gain).astype(BF16)

    cq_ref[...] = rms(_dot(h, w_ref[:, :MLA_Q_RANK]), gq_ref[...])
    ckv_ref[...] = rms(_dot(h, w_ref[:, MLA_Q_RANK:MLA_Q_RANK + MLA_KV_RANK]), gkv_ref[...])
    zr = _dot(h, w_ref[:, MLA_Q_RANK + MLA_KV_RANK:])
    kr_ref[...] = _rope128(zr, cos_ref[...], sin_ref[...])[:, :MLA_ROPE].astype(BF16)


def _mla_down(dims, x, g, mod, w_pad, gq, gkv, cos, sin, tm):
    T, D = dims.T, dims.D
    n = w_pad.shape[1]
    return pl.pallas_call(
        _mla_down_body,
        grid=(T // tm,),
        in_specs=[_row_spec(tm, D), _const_spec((1, D)), _mod_spec(dims, tm, 0, D), _mod_spec(dims, tm, 1, D),
                  _const_spec((D, n)), _const_spec((1, MLA_Q_RANK)), _const_spec((1, MLA_KV_RANK)),
                  _row_spec(tm, LANES), _row_spec(tm, LANES)],
        out_specs=[_row_spec(tm, MLA_Q_RANK), _row_spec(tm, MLA_KV_RANK), _row_spec(tm, MLA_ROPE)],
        out_shape=[jax.ShapeDtypeStruct((T, MLA_Q_RANK), BF16), jax.ShapeDtypeStruct((T, MLA_KV_RANK), BF16),
                   jax.ShapeDtypeStruct((T, MLA_ROPE), BF16)],
        compiler_params=_cparams("parallel"),
        name="mla_down",
    )(x, g, mod, mod, w_pad, gq, gkv, cos, sin)


def _mla_up_body(cq_ref, ckv_ref, kr_ref, cos_ref, sin_ref, wqn_ref, wqr_ref, wk_ref, wvt_ref,
                 q_ref, k_ref, vt_ref):
    cq, ckv, kr = cq_ref[...], ckv_ref[...], kr_ref[...]
    cos, sin = cos_ref[...], sin_ref[...]
    scale = MLA_QK ** -0.5 * LOG2E
    H = MLA_HEADS
    tm = cq.shape[0]
    ones_rows = jnp.where(lax.broadcasted_iota(I32, (MLA_VT_ROWS - MLA_V, tm), 0) == 0, 1.0, 0.0).astype(BF16)
    for j in range(H // 2):
        zq = _dot(cq, wqn_ref[:, j * MXU_COLS:(j + 1) * MXU_COLS]) * scale
        zk = _dot(ckv, wk_ref[:, j * MXU_COLS:(j + 1) * MXU_COLS])
        for c in range(2):
            hh = 2 * j + c
            q_ref[hh, :, :MLA_NOPE] = zq[:, c * LANES:(c + 1) * LANES].astype(BF16)
            k_ref[hh, :, :MLA_NOPE] = zk[:, c * LANES:(c + 1) * LANES].astype(BF16)
            k_ref[hh, :, MLA_NOPE:] = kr
            vt_ref[hh, :MLA_V, :] = _dot_nt(wvt_ref[hh * MLA_V:(hh + 1) * MLA_V, :], ckv).astype(BF16)
            vt_ref[hh, MLA_V:, :] = ones_rows
    for j in range(H // 4):
        zr = _dot(cq, wqr_ref[:, j * MXU_COLS:(j + 1) * MXU_COLS])
        for c in range(2):
            r = (_rope128(zr[:, c * LANES:(c + 1) * LANES], cos, sin) * scale).astype(BF16)
            q_ref[4 * j + 2 * c, :, MLA_NOPE:] = r[:, :MLA_ROPE]
            q_ref[4 * j + 2 * c + 1, :, MLA_NOPE:] = r[:, MLA_ROPE:]


def _mla_up(dims, cq, ckv, kr, cos, sin, wqn, wqr, wk, wvt, tm):
    T = dims.T
    H = MLA_HEADS

    def hspec(n):
        return pl.BlockSpec((H, tm, n), lambda i: (0, i, 0))

    return pl.pallas_call(
        _mla_up_body,
        grid=(T // tm,),
        in_specs=[_row_spec(tm, MLA_Q_RANK), _row_spec(tm, MLA_KV_RANK), _row_spec(tm, MLA_ROPE),
                  _row_spec(tm, LANES), _row_spec(tm, LANES),
                  _const_spec(wqn.shape), _const_spec(wqr.shape), _const_spec(wk.shape), _const_spec(wvt.shape)],
        out_specs=[hspec(MLA_QK), hspec(MLA_QK), pl.BlockSpec((H, MLA_VT_ROWS, tm), lambda i: (0, 0, i))],
        out_shape=[jax.ShapeDtypeStruct((H, T, MLA_QK), BF16), jax.ShapeDtypeStruct((H, T, MLA_QK), BF16),
                   jax.ShapeDtypeStruct((H, MLA_VT_ROWS, T), BF16)],
        compiler_params=_cparams("parallel"),
        name="mla_up",
    )(cq, ckv, kr, cos, sin, wqn, wqr, wk, wvt)


def _mla_attn_body(*refs, tk, n_chunks, latent_keys):
    if latent_keys:
        q_ref, kc_ref, vc_ref, kl_ref, vl_ref, o_ref, acc_ref = refs
    else:
        q_ref, kc_ref, vc_ref, o_ref, acc_ref = refs
    q = q_ref[0]

    def scores(c):
        return _dot_nt(kl_ref[0, c * tk:(c + 1) * tk, :], q).astype(BF16)

    s = _dot_nt(kc_ref[0], q).astype(BF16)
    ahead = [scores(c) for c in range(min(MLA_LOOKAHEAD, n_chunks))] if latent_keys else []
    m = jnp.max(s, axis=0, keepdims=True)
    acc_ref[...] = _dot(vc_ref[0], jnp.exp2(s - m))
    if latent_keys:
        for c in range(n_chunks):
            s = ahead.pop(0)
            if c + MLA_LOOKAHEAD < n_chunks:
                ahead.append(scores(c + MLA_LOOKAHEAD))
            m2 = jnp.maximum(m, jnp.max(s, axis=0, keepdims=True))
            rescale = jnp.exp2(m - m2).astype(F32)
            acc_ref[...] = rescale * acc_ref[...] + _dot(vl_ref[0, :, c * tk:(c + 1) * tk], jnp.exp2(s - m2))
            m = m2
    acc = acc_ref[...]
    o_ref[...] = (acc[:MLA_V] / acc[MLA_V:MLA_V + 1]).T.astype(BF16)


def _mla_attn(dims, q, k, vt, *, latent_queries, tq, tk):
    B, L, C = dims.B, dims.L, dims.C
    H = MLA_HEADS
    if latent_queries:
        nq, row0, n_out = L // tq, lambda b, n: b * (L // tq) + n, B * L
    else:
        nq, row0, n_out = 1, lambda b, n: (B * L) // tq + b, B * C
    out_row0 = row0 if latent_queries else (lambda b, n: b)

    def q_map(b, h, n):
        return (h, row0(b, n), 0)

    def o_map(b, h, n):
        return (out_row0(b, n), h)

    in_specs = [pl.BlockSpec((1, tq, MLA_QK), q_map),
                pl.BlockSpec((1, C, MLA_QK), lambda b, h, n: (h, (B * L) // C + b, 0)),
                pl.BlockSpec((1, MLA_VT_ROWS, C), lambda b, h, n: (h, 0, (B * L) // C + b))]
    args = [q, k, vt]
    if latent_queries:
        in_specs += [pl.BlockSpec((1, L, MLA_QK), lambda b, h, n: (h, b, 0)),
                     pl.BlockSpec((1, MLA_VT_ROWS, L), lambda b, h, n: (h, 0, b))]
        args += [k, vt]
    return pl.pallas_call(
        functools.partial(_mla_attn_body, tk=tk, n_chunks=L // tk, latent_keys=latent_queries),
        grid=(B, H, nq),
        in_specs=in_specs,
        out_specs=pl.BlockSpec((tq, MLA_V), o_map),
        out_shape=jax.ShapeDtypeStruct((n_out, H * MLA_V), BF16),
        scratch_shapes=[pltpu.VMEM((MLA_VT_ROWS, tq), F32)],
        compiler_params=_cparams("parallel", "parallel", "arbitrary"),
        name="mla_attn_latent" if latent_queries else "mla_attn_ctx",
    )(*args)


def _router_body(x_ref, g_ref, sh_ref, sc_ref, whi_ref, wlo_ref, br_ref, hn_ref, rho_ref, gate_ref, cnt_ref):
    h = _norm_mod(x_ref[...], g_ref[...], sh_ref[0], sc_ref[0])
    hi = h.astype(BF16)
    lo = (h - hi.astype(F32)).astype(BF16)
    hn_ref[...] = hi
    whi, wlo = whi_ref[...], wlo_ref[...]
    lt = _dot_nt(whi, hi) + _dot_nt(wlo, hi) + _dot_nt(whi, lo) + br_ref[...]
    E = N_EXPERTS
    eidx = lax.broadcasted_iota(I32, (E, SUB), 0)
    onehots, tops = [], []
    for _ in range(TOP_K):
        m = jnp.max(lt, axis=0, keepdims=True)
        idx = jnp.min(jnp.where(lt == m, eidx, E), axis=0, keepdims=True)
        oh = eidx == idx
        onehots.append(oh)
        tops.append(m)
        lt = jnp.where(oh, -jnp.inf, lt)
    ex = [jnp.exp(t - tops[0]) for t in tops]
    den = ex[0] + ex[1] + ex[2] + ex[3]
    chosen = (onehots[0] | onehots[1] | onehots[2] | onehots[3])
    chosen_f = jnp.where(chosen, 1.0, 0.0)
    cnt = jnp.sum(chosen_f, axis=1, keepdims=True)
    units = jnp.floor((cnt + (UNIT - 1)) / UNIT)
    cnt_ref[0] = units
    lower = (lax.broadcasted_iota(I32, (E, E), 1) < lax.broadcasted_iota(I32, (E, E), 0))
    start = _dot(jnp.where(lower, 1.0, 0.0).astype(BF16),
                 jnp.broadcast_to(units, (E, LANES)).astype(BF16))[:, :1] * UNIT
    before = (lax.broadcasted_iota(I32, (SUB, SUB), 0) < lax.broadcasted_iota(I32, (SUB, SUB), 1))
    rank = _dot(chosen_f.astype(BF16), jnp.where(before, 1.0, 0.0).astype(BF16))
    row = start + rank
    for k in range(TOP_K):
        rho_ref[k:k + 1, :] = jnp.sum(jnp.where(onehots[k], row, 0.0), axis=0, keepdims=True).astype(I32)
        gate_ref[k:k + 1, :] = ex[k] / den


def _router(dims, x, g, mod, whi, wlo, br, n_rows):
    T, D = n_rows, dims.D
    n_sub = T // SUB
    E = N_EXPERTS
    return pl.pallas_call(
        _router_body,
        grid=(n_sub,),
        in_specs=[_row_spec(SUB, D), _const_spec((1, D)), _mod_spec(dims, SUB, 3, D), _mod_spec(dims, SUB, 4, D),
                  _const_spec((E, D)), _const_spec((E, D)), _const_spec((E, 1))],
        out_specs=[_row_spec(SUB, D), pl.BlockSpec((TOP_K, SUB), lambda i: (0, i)),
                   pl.BlockSpec((TOP_K, SUB), lambda i: (0, i)), pl.BlockSpec((1, E, 1), lambda i: (i, 0, 0))],
        out_shape=[jax.ShapeDtypeStruct((T, D), BF16), jax.ShapeDtypeStruct((TOP_K, T), I32),
                   jax.ShapeDtypeStruct((TOP_K, T), F32), jax.ShapeDtypeStruct((n_sub, E, 1), F32)],
        compiler_params=_cparams("parallel"),
        name="moe_router",
    )(x, g, mod, mod, whi, wlo, br)


def _dispatch_body(hn_ref, rho_ref, a_ref, *, n_sub):
    u = pl.program_id(0)
    rows = lax.broadcasted_iota(I32, (SUB_ROWS, SUB), 0)
    hit = rows == rho_ref[0:1, :]
    for k in range(1, TOP_K):
        hit = hit | (rows == rho_ref[k:k + 1, :])
    sel = jnp.where(hit & (u < n_sub), 1.0, 0.0).astype(BF16)
    a_ref[...] = _dot(sel, hn_ref[...]).astype(BF16)


def _dispatch(dims, hn, rho):
    T, D = hn.shape
    n_sub = T // SUB
    last = n_sub - 1
    return pl.pallas_call(
        functools.partial(_dispatch_body, n_sub=n_sub),
        grid=(n_sub + 1,),
        in_specs=[pl.BlockSpec((SUB, D), lambda u: (jnp.minimum(u, last), 0)),
                  pl.BlockSpec((TOP_K, SUB), lambda u: (0, jnp.minimum(u, last)))],
        out_specs=pl.BlockSpec((SUB_ROWS, D), lambda u: (u, 0)),
        out_shape=jax.ShapeDtypeStruct(((n_sub + 1) * SUB_ROWS, D), BF16),
        compiler_params=_cparams("parallel"),
        name="moe_dispatch",
    )(hn, rho)


def _cast_block(w1f_ref, w2f_ref, w1o_ref, w2o_ref):
    w1o_ref[...] = w1f_ref[...].astype(BF16)
    w2o_ref[...] = w2f_ref[...].astype(BF16)


def _cast_steps(n_blocks):
    steps = 1
    while steps * 2 <= min(n_blocks, CAST_STEPS_MAX):
        steps *= 2
    return steps


def _cast_specs(steps, n_rows1, ff2, n_rows2, D, index):
    r1, r2 = n_rows1 // steps, n_rows2 // steps
    assert r1 * steps == n_rows1 and r2 * steps == n_rows2 and r1 % UNIT == 0 and r2 % UNIT == 0
    return [pl.BlockSpec((r1, ff2), index), pl.BlockSpec((r2, D), index)]


def _cast_weights(w1f, w2f, layer, steps):
    ff2, D = w1f.shape[1], w2f.shape[1]
    n1, n2 = N_EXPERTS * D, N_EXPERTS * (ff2 // 2)
    in_specs = _cast_specs(steps, n1, ff2, n2, D, lambda j: (layer * steps + j, 0))
    out_specs = _cast_specs(steps, n1, ff2, n2, D, lambda j: (j, 0))
    return pl.pallas_call(
        _cast_block,
        grid=(steps,),
        in_specs=in_specs, out_specs=out_specs,
        out_shape=[jax.ShapeDtypeStruct((n1, ff2), BF16), jax.ShapeDtypeStruct((n2, D), BF16)],
        compiler_params=_cparams("parallel"),
        name="expert_weight_cast",
    )(w1f, w2f)


def _expert_body(src_ref, be_ref, nu_ref, a_ref, w1_ref, b1_ref, w2_ref, b2_ref, *rest, cast_steps):
    cast_next = cast_steps > 0
    if cast_next:
        w1f_ref, w2f_ref, y_ref, w1o_ref, w2o_ref, xbuf_ref, sem_ref = rest
    else:
        y_ref, xbuf_ref, sem_ref = rest
    j = pl.program_id(0)
    last = pl.num_programs(0) - 1

    def start_all(jj, slot):
        for i in range(BLOCK_UNITS):
            unit = src_ref[jj * BLOCK_UNITS + i]
            pltpu.make_async_copy(a_ref.at[pl.ds(pl.multiple_of(unit * UNIT, UNIT), UNIT)],
                                  xbuf_ref.at[slot, pl.ds(i * UNIT, UNIT)], sem_ref.at[slot]).start()

    def wait_slot(slot):
        pltpu.make_async_copy(a_ref.at[pl.ds(0, EXPERT_BLOCK)], xbuf_ref.at[slot], sem_ref.at[slot]).wait()

    @pl.when(j == 0)
    def _():
        start_all(0, 0)

    slot = j % 2
    wait_slot(slot)
    start_all(jnp.minimum(j + 1, last), (j + 1) % 2)

    if cast_next:
        @pl.when(j < cast_steps)
        def _():
            _cast_block(w1f_ref, w2f_ref, w1o_ref, w2o_ref)

    @pl.when(j < nu_ref[0])
    def _():
        a = _dot(xbuf_ref[slot], w1_ref[...]) + b1_ref[0]
        a_glu = jnp.minimum(a[:, :EXPERT_FF], SWIGLU_LIMIT)
        a_lin = jnp.clip(a[:, EXPERT_FF:], -SWIGLU_LIMIT, SWIGLU_LIMIT)
        u = a_glu * jax.nn.sigmoid(SWIGLU_ALPHA * a_glu) * (a_lin + 1.0)
        y_ref[...] = (_dot(u.astype(BF16), w2_ref[...]) + b2_ref[0]).astype(BF16)

    @pl.when(j >= nu_ref[0])
    def _():
        y_ref[...] = jnp.zeros_like(y_ref)

    @pl.when(j == last)
    def _():
        wait_slot((j + 1) % 2)


def _experts(a, src_unit, blk_e, n_used, layer, w1, b1, w2, b2, n_blocks, w1f, w2f):
    D = a.shape[1]
    ff2 = w1.shape[1]
    cast_next = w1f is not None
    steps = _cast_steps(n_blocks)

    def be_map(j, src, be, nu):
        return (layer * N_EXPERTS + be[j], 0, 0)

    def w_map(j, src, be, nu):
        return (be[j], 0)

    def next_map(j, src, be, nu):
        return ((layer + 1) * steps + jnp.minimum(j, steps - 1), 0)

    def out_map(j, src, be, nu):
        return (jnp.minimum(j, steps - 1), 0)

    n1, n2 = N_EXPERTS * D, N_EXPERTS * EXPERT_FF
    in_specs = [pl.BlockSpec(memory_space=pl.ANY),
                pl.BlockSpec((D, ff2), w_map), pl.BlockSpec((1, 1, ff2), be_map),
                pl.BlockSpec((EXPERT_FF, D), w_map), pl.BlockSpec((1, 1, D), be_map)]
    out_specs = [pl.BlockSpec((EXPERT_BLOCK, D), lambda j, src, be, nu: (j, 0))]
    out_shape = [jax.ShapeDtypeStruct((n_blocks * EXPERT_BLOCK, D), BF16)]
    args = [a, w1, b1, w2, b2]
    if cast_next:
        in_specs += _cast_specs(steps, n1, ff2, n2, D, next_map)
        out_specs += _cast_specs(steps, n1, ff2, n2, D, out_map)
        out_shape += [jax.ShapeDtypeStruct((n1, ff2), BF16), jax.ShapeDtypeStruct((n2, D), BF16)]
        args += [w1f, w2f]
    grid_spec = pltpu.PrefetchScalarGridSpec(
        num_scalar_prefetch=3,
        grid=(n_blocks,),
        in_specs=in_specs,
        out_specs=out_specs,
        scratch_shapes=[pltpu.VMEM((2, EXPERT_BLOCK, D), BF16), pltpu.SemaphoreType.DMA((2,))],
    )
    return pl.pallas_call(
        functools.partial(_expert_body, cast_steps=steps if cast_next else 0),
        grid_spec=grid_spec,
        out_shape=out_shape,
        compiler_params=_cparams("arbitrary"),
        name="moe_experts",
    )(src_unit, blk_e, n_used, *args)


def _combine_body(dst_ref, x_ref, rho_ref, gate_ref, g2_ref, *rest, n_sub, final):
    if final:
        fg_ref, ys_ref, o_ref, buf_ref, sem_ref = rest
    else:
        ys_ref, o_ref, buf_ref, sem_ref = rest
    u = pl.program_id(0)

    def start_all(uu, slot):
        for i in range(SUB_UNITS):
            d = jnp.maximum(dst_ref[uu * SUB_UNITS + i], 0)
            pltpu.make_async_copy(ys_ref.at[pl.ds(pl.multiple_of(d * UNIT, UNIT), UNIT)],
                                  buf_ref.at[slot, pl.ds(i * UNIT, UNIT)], sem_ref.at[slot]).start()

    def wait_slot(slot):
        pltpu.make_async_copy(ys_ref.at[pl.ds(0, SUB_ROWS)], buf_ref.at[slot], sem_ref.at[slot]).wait()

    @pl.when(u == 0)
    def _():
        start_all(0, 0)

    slot = u % 2
    wait_slot(slot)
    start_all(jnp.minimum(u + 1, n_sub - 1), (u + 1) % 2)
    cols = lax.broadcasted_iota(I32, (SUB, SUB_ROWS), 1)
    wgt = jnp.where(cols == rho_ref[:, 0:1], gate_ref[:, 0:1], 0.0)
    for k in range(1, TOP_K):
        wgt = wgt + jnp.where(cols == rho_ref[:, k:k + 1], gate_ref[:, k:k + 1], 0.0)
    o = x_ref[...] + g2_ref[0] * _dot(wgt.astype(BF16), buf_ref[slot])
    if final:
        o = o * lax.rsqrt(jnp.mean(o * o, axis=-1, keepdims=True) + EPS) * fg_ref[...]
    o_ref[...] = o

    @pl.when(u == n_sub - 1)
    def _():
        wait_slot((u + 1) % 2)


def _combine(dims, dst_unit, x, rho_t, gate_t, mod, ys, n_rows, final_g):
    D = dims.D
    n_sub = n_rows // SUB
    final = final_g is not None
    in_specs = [pl.BlockSpec((SUB, D), lambda u, dst: (u, 0)),
                pl.BlockSpec((SUB, TOP_K), lambda u, dst: (u, 0)),
                pl.BlockSpec((SUB, TOP_K), lambda u, dst: (u, 0)),
                pl.BlockSpec((1, 1, D), lambda u, dst: (dims.mod_row(SUB)(u) * 6 + 5, 0, 0))]
    args = [x, rho_t, gate_t, mod]
    if final:
        in_specs.append(pl.BlockSpec((1, D), lambda u, dst: (0, 0)))
        args.append(final_g)
    grid_spec = pltpu.PrefetchScalarGridSpec(
        num_scalar_prefetch=1,
        grid=(n_sub,),
        in_specs=in_specs + [pl.BlockSpec(memory_space=pl.ANY)],
        out_specs=pl.BlockSpec((SUB, D), lambda u, dst: (u, 0)),
        scratch_shapes=[pltpu.VMEM((2, SUB_ROWS, D), BF16), pltpu.SemaphoreType.DMA((2,))],
    )
    return pl.pallas_call(
        functools.partial(_combine_body, n_sub=n_sub, final=final),
        grid_spec=grid_spec,
        out_shape=jax.ShapeDtypeStruct((n_rows if final else x.shape[0], D), F32),
        input_output_aliases={} if final else {1: 0},
        compiler_params=_cparams("arbitrary"),
        name="moe_combine",
    )(dst_unit, *args, ys)


def _moe_tables(units, n_blocks):
    n_sub, E = units.shape
    g = units.astype(I32)
    off_a = jnp.cumsum(g, axis=1) - g
    tot = jnp.sum(g, axis=0)
    pad = (tot + BLOCK_UNITS - 1) // BLOCK_UNITS * BLOCK_UNITS
    pend = jnp.cumsum(pad)
    pos_x = (pend - pad)[None, :] + jnp.cumsum(g, axis=0) - g
    sub_base = jnp.arange(n_sub, dtype=I32)[:, None] * SUB_UNITS + off_a
    p = jnp.arange(n_blocks * BLOCK_UNITS, dtype=I32)[:, None]
    lo, cnt, delta = pos_x.reshape(1, -1), g.reshape(1, -1), (sub_base - pos_x).reshape(1, -1)
    inside = (p >= lo) & (p < lo + cnt)
    src_unit = jnp.where(jnp.any(inside, axis=1), jnp.sum(jnp.where(inside, delta + p, 0), axis=1),
                         n_sub * SUB_UNITS).astype(I32)
    blk_e = jnp.minimum(jnp.sum(jnp.arange(n_blocks, dtype=I32)[:, None] * BLOCK_UNITS >= pend[None, :], axis=1),
                        E - 1).astype(I32)
    n_used = (pend[-1:] // BLOCK_UNITS).astype(I32)
    i = jnp.arange(SUB_UNITS, dtype=I32)[None, :, None]
    inside = (i >= off_a[:, None, :]) & (i < (off_a + g)[:, None, :])
    dst = jnp.sum(jnp.where(inside, (pos_x - off_a)[:, None, :] + i, 0), axis=-1)
    dst_unit = jnp.where(jnp.any(inside, axis=-1), dst, -1).astype(I32).reshape(-1)
    return src_unit, blk_e, n_used, dst_unit


def _moe(dims, x, g, mod, whi, wlo, br, layer, w1, b1, w2, b2, w1f, w2f, n_rows, final_g):
    n_sub = n_rows // SUB
    n_blocks = (n_sub * SUB_UNITS + N_EXPERTS * (BLOCK_UNITS - 1)) // BLOCK_UNITS + 1
    hn, rho, gate, units = _router(dims, x, g, mod, whi, wlo, br, n_rows)
    src_unit, blk_e, n_used, dst_unit = _moe_tables(units.reshape(n_sub, N_EXPERTS), n_blocks)
    a = _dispatch(dims, hn, rho)
    outs = _experts(a, src_unit, blk_e, n_used, layer, w1, b1, w2, b2, n_blocks, w1f, w2f)
    x = _combine(dims, dst_unit, x, rho.T, gate.T, mod, outs[0], n_rows, final_g)
    return x, (outs[1:] if w1f is not None else None)


def _rope_tables(B, L, C):
    half = GQA_HEAD_DIM // 4
    inv = ROPE_BASE ** (-jnp.arange(half, dtype=F32) / half)
    t = jnp.arange(L)
    ang_r = (t // GRID_W).astype(F32)[:, None] * inv[None, :]
    ang_c = (t % GRID_W).astype(F32)[:, None] * inv[None, :]
    cos = jnp.concatenate([jnp.cos(ang_r)] * 2 + [jnp.cos(ang_c)] * 2, axis=-1)
    sin = jnp.concatenate([-jnp.sin(ang_r), jnp.sin(ang_r), -jnp.sin(ang_c), jnp.sin(ang_c)], axis=-1)
    cos = jnp.tile(cos, (B, LANES // cos.shape[1]))
    sin = jnp.tile(sin, (B, LANES // sin.shape[1]))
    cos = jnp.concatenate([cos, jnp.ones((B * C, LANES), F32)], axis=0)
    sin = jnp.concatenate([sin, jnp.zeros((B * C, LANES), F32)], axis=0)
    return cos, sin


def kernel(x, c, ctx, c_ctx, ada_w, ada_b, norm_g, final_g, gqa_w_qkv, gqa_b_qkv, gqa_sink, gqa_w_o, gqa_b_o,
           pool_w, pool_b, pool_scale, mla_w_down, mla_g_q, mla_w_uq, mla_g_kv, mla_w_ukv, mla_w_o,
           router_w, router_b, exp_w1, exp_b1, exp_w2, exp_b2):
    B, L, D = x.shape
    C = ctx.shape[1]
    depth = ada_w.shape[0]
    dims = _Dims(B, L, C, D)
    T = dims.T
    tm = 512 if (L % 512 == 0 and (B * C) % 512 == 0) else 256
    assert L % tm == 0 and (B * C) % tm == 0 and L % SUB == 0 and C % SUB == 0 and B + 1 <= UNIT
    assert L % GRID_W == 0 and SUB % WINDOW == 0 and C % WINDOW == 0

    cc = jnp.zeros((UNIT, D), F32).at[:B].set(c).at[B].set(c_ctx)
    mods = _ada_mod(cc, ada_w, ada_b)
    cos, sin = _rope_tables(B, L, C)
    xs = jnp.concatenate([x.reshape(B * L, D), ctx.reshape(B * C, D)], axis=0)
    ne = depth * N_EXPERTS
    w1f = exp_w1.reshape(ne * D, -1)
    w2f = exp_w2.reshape(-1, D)
    b1 = exp_b1.reshape(ne, 1, -1)
    b2 = exp_b2.reshape(ne, 1, D)
    n_blocks_all = ((T // SUB) * SUB_UNITS + N_EXPERTS * (BLOCK_UNITS - 1)) // BLOCK_UNITS + 1
    w_bf16 = _cast_weights(w1f, w2f, 0, _cast_steps(n_blocks_all))

    for i in range(depth):
        need_ctx = i < depth - 1
        mod = mods[i].reshape(UNIT * 6, 1, D)
        kind, j = i % N_MIXERS, i // N_MIXERS
        g1 = norm_g[i, 0].reshape(1, D)
        if kind == 0:
            nqk = (GQA_HEADS + GQA_KV_HEADS) * GQA_HEAD_DIM
            q, k, vt = _gqa_qkv(dims, xs, g1, mod, gqa_w_qkv[j][:, :nqk].astype(BF16),
                                gqa_b_qkv[j][:nqk].reshape(1, -1), gqa_w_qkv[j][:, nqk:].T.astype(BF16),
                                gqa_b_qkv[j][nqk:].reshape(-1, 1), cos, sin, tm)
            o = _gqa_attn(dims, q, k, vt, gqa_sink[j], local=True, tq=SUB)
            oc = _gqa_attn(dims, q, k, vt, gqa_sink[j], local=False, tq=C) if need_ctx else None
            xs = _proj_res(dims, o, oc, gqa_w_o[j].astype(BF16), gqa_b_o[j].reshape(1, D), xs, mod, 2, tm)
        elif kind == 1:
            xs = _pool(dims, xs, g1, mod, pool_w[j].astype(BF16), pool_b[j].reshape(1, D),
                       pool_scale[j].reshape(1, D), SUB)
        else:
            H = MLA_HEADS
            w_down = jnp.pad(mla_w_down[j], ((0, 0), (0, LANES - MLA_ROPE))).astype(BF16)
            wq = mla_w_uq[j].reshape(MLA_Q_RANK, H, MLA_QK)
            wqn = wq[:, :, :MLA_NOPE].reshape(MLA_Q_RANK, H * MLA_NOPE).astype(BF16)
            wqr = wq[:, :, MLA_NOPE:].reshape(MLA_Q_RANK, H * MLA_ROPE).astype(BF16)
            wkv = mla_w_ukv[j].reshape(MLA_KV_RANK, H, MLA_NOPE + MLA_V)
            wk = wkv[:, :, :MLA_NOPE].reshape(MLA_KV_RANK, H * MLA_NOPE).astype(BF16)
            wvt = wkv[:, :, MLA_NOPE:].reshape(MLA_KV_RANK, H * MLA_V).T.astype(BF16)
            cq, ckv, kr = _mla_down(dims, xs, g1, mod, w_down, mla_g_q[j].reshape(1, -1),
                                    mla_g_kv[j].reshape(1, -1), cos, sin, tm)
            q, k, vt = _mla_up(dims, cq, ckv, kr, cos, sin, wqn, wqr, wk, wvt, SUB)
            o = _mla_attn(dims, q, k, vt, latent_queries=True, tq=tm, tk=min(1024, L))
            oc = _mla_attn(dims, q, k, vt, latent_queries=False, tq=C, tk=min(1024, L)) if need_ctx else None
            xs = _proj_res(dims, o, oc, mla_w_o[j].astype(BF16), jnp.zeros((1, D), F32), xs, mod, 2, tm)
        wr_t = router_w[i].T
        whi = wr_t.astype(BF16)
        wlo = (wr_t - whi.astype(F32)).astype(BF16)
        last = i == depth - 1
        xs, w_bf16 = _moe(dims, xs, norm_g[i, 1].reshape(1, D), mod, whi, wlo, router_b[i].reshape(-1, 1), i,
                          w_bf16[0], b1, w_bf16[1], b2, None if last else w1f, None if last else w2f,
                          B * L if last else T, final_g.reshape(1, D) if last else None)
    return xs.reshape(B, L, D)
```
